```python
import math
import jax, jax.numpy as jnp
from jax import lax
import numpy as np

D_MODEL = 4096
BATCH = 2
SEQ = 8192
DEPTH = 4

GRID_W = 64
CTX_LEN = 256
N_MIXERS = 3
N_POOL_LAYERS = (DEPTH + 2) // 3
N_ATTN_LAYERS = (DEPTH + 1) // 3
N_RET_LAYERS = DEPTH // 3

POOL_EXPAND = 2
POOL_WIDTH = POOL_EXPAND * D_MODEL
POOL_WINDOWS = (2, 4, 8, 16)
POOL_GROUP = POOL_WIDTH // len(POOL_WINDOWS)

ATTN_HEAD_DIM = 128
ATTN_Q_HEADS = D_MODEL // ATTN_HEAD_DIM
ATTN_KV_HEADS = 8
ATTN_GROUPS = ATTN_Q_HEADS // ATTN_KV_HEADS
ATTN_WIDTH = ATTN_Q_HEADS * ATTN_HEAD_DIM
ATTN_KV_WIDTH = ATTN_KV_HEADS * ATTN_HEAD_DIM
ATTN_IN = 2 * ATTN_WIDTH + 2 * ATTN_KV_WIDTH
Q_BLOCK = 128

RET_HEADS = 16
RET_QK_DIM = D_MODEL // RET_HEADS
RET_V_DIM = 2 * RET_QK_DIM
RET_QK_WIDTH = RET_HEADS * RET_QK_DIM
RET_V_WIDTH = RET_HEADS * RET_V_DIM
RET_IN = 2 * RET_QK_WIDTH + 2 * RET_V_WIDTH
RET_CHUNK = 128

ROPE_THETA = 10000.0
NORM_EPS = 1e-6
GN_EPS = 1e-5

kernel_name = "hybrid_pool_gqa_retention_prefix_dit"


def rms_norm(x, g):
    xf = x.astype(jnp.float32)
    y = xf * lax.rsqrt(jnp.mean(xf * xf, axis=-1, keepdims=True) + NORM_EPS)
    return (y * g.astype(jnp.float32)).astype(x.dtype)


def modulate(h, shift, scale):
    return h * (1.0 + scale) + shift


def axial_rope(n_tokens, head_dim):
    rows = n_tokens // GRID_W
    row = jnp.repeat(jnp.arange(rows, dtype=jnp.float32), GRID_W)
    col = jnp.tile(jnp.arange(GRID_W, dtype=jnp.float32), rows)
    n_freq = head_dim // 4
    inv = ROPE_THETA ** (-jnp.arange(n_freq, dtype=jnp.float32) / n_freq)
    ang = jnp.concatenate([row[:, None] * inv, col[:, None] * inv], axis=-1)
    return jnp.cos(ang), jnp.sin(ang)


def apply_rope(x, cos, sin):
    b, n, h, d = x.shape
    xf = x.astype(jnp.float32).reshape(b, n, h, d // 2, 2)
    x0, x1 = xf[..., 0], xf[..., 1]
    cs, sn = cos[None, :, None, :], sin[None, :, None, :]
    out = jnp.stack([x0 * cs - x1 * sn, x0 * sn + x1 * cs], axis=-1)
    return out.reshape(b, n, h, d).astype(x.dtype)


def centred_mean_minus_self(u, window):
    n = u.shape[1]
    uf = u.astype(jnp.float32)
    cs = jnp.pad(jnp.cumsum(uf, axis=1), ((0, 0), (1, 0), (0, 0)))
    t = jnp.arange(n)
    lo = jnp.clip(t - window // 2, 0, n)
    hi = jnp.clip(t + window - window // 2, 0, n)
    s = jnp.take(cs, hi, axis=1) - jnp.take(cs, lo, axis=1)
    cnt = (hi - lo).astype(jnp.float32)
    return (s / cnt[None, :, None] - uf).astype(u.dtype)


def pool_branch(h, w_in, w_grp, scale, w_out):
    b, n, _ = h.shape
    u, g = jnp.split(h @ w_in, 2, axis=-1)
    ug = u.reshape(b, n, len(POOL_WINDOWS), POOL_GROUP)
    pooled = jnp.stack([centred_mean_minus_self(ug[:, :, gi], w)
                        for gi, w in enumerate(POOL_WINDOWS)], axis=2)
    mixed = jnp.einsum('bngc,gcd->bngd', pooled, w_grp).reshape(b, n, POOL_WIDTH) * scale
    return (mixed * jax.nn.silu(g)) @ w_out


def attend(q, k, v):
    s = jnp.einsum('bqhgd,bshd->bhgqs', q.astype(jnp.float32), k.astype(jnp.float32)) * (ATTN_HEAD_DIM ** -0.5)
    p = jax.nn.softmax(s, axis=-1)
    return jnp.einsum('bhgqs,bshd->bqhgd', p, v.astype(jnp.float32)).astype(v.dtype)


def attention_mixer(h, hc, w_in, q_g, k_g, w_out, need_ctx):
    b, n, _ = h.shape
    l = hc.shape[1]

    def project(t, rope):

        m = t.shape[1]
        q, k, v, g = jnp.split(t @ w_in, [ATTN_WIDTH, ATTN_WIDTH + ATTN_KV_WIDTH,
                                          ATTN_WIDTH + 2 * ATTN_KV_WIDTH], axis=-1)
        q = rms_norm(q.reshape(b, m, ATTN_Q_HEADS, ATTN_HEAD_DIM), q_g)
        k = rms_norm(k.reshape(b, m, ATTN_KV_HEADS, ATTN_HEAD_DIM), k_g)
        if rope is not None:
            q = apply_rope(q, *rope)
            k = apply_rope(k, *rope)
        q = q.reshape(b, m, ATTN_KV_HEADS, ATTN_GROUPS, ATTN_HEAD_DIM)
        return q, k, v.reshape(b, m, ATTN_KV_HEADS, ATTN_HEAD_DIM), g

    q, k, v, g = project(h, axial_rope(n, ATTN_HEAD_DIM))
    qc, kc, vc, gc = project(hc, None)
    k_all = jnp.concatenate([kc, k], axis=1)
    v_all = jnp.concatenate([vc, v], axis=1)
    q_blocks = q.reshape(b, n // Q_BLOCK, Q_BLOCK, ATTN_KV_HEADS, ATTN_GROUPS, ATTN_HEAD_DIM)
    q_blocks = q_blocks.transpose(1, 0, 2, 3, 4, 5)
    o = lax.map(lambda qb: attend(qb, k_all, v_all), q_blocks)
    o = o.transpose(1, 0, 2, 3, 4, 5).reshape(b, n, ATTN_WIDTH)
    y = (o * jax.nn.silu(g)) @ w_out
    yc = None
    if need_ctx:
        oc = attend(qc, kc, vc).reshape(b, l, ATTN_WIDTH)
        yc = (oc * jax.nn.silu(gc)) @ w_out
    return y, yc


def chunk_retention(q, k, v, log_g, s0, include_diag):
    b, hh, n, dk = q.shape
    dv = v.shape[-1]
    nc = n // RET_CHUNK
    i = jnp.arange(RET_CHUNK, dtype=jnp.float32)
    diff = i[:, None] - i[None, :]
    mask = (diff >= 0) if include_diag else (diff > 0)
    decay_in = jnp.exp(jnp.where(mask, diff, 0.0)[None] * log_g[:, None, None]) * mask
    q_dec = jnp.exp((i + 1.0)[None, :] * log_g[:, None])[..., None]
    k_dec = jnp.exp((RET_CHUNK - 1.0 - i)[None, :] * log_g[:, None])[..., None]
    chunk_dec = jnp.exp(RET_CHUNK * log_g)[:, None, None]

    def to_chunks(a):
        return a.astype(jnp.float32).reshape(b, hh, nc, RET_CHUNK, a.shape[-1]).transpose(2, 0, 1, 3, 4)

    def step(s, inp):
        qc, kc, vc = inp
        scores = jnp.einsum('bhid,bhjd->bhij', qc, kc) * decay_in
        o = jnp.einsum('bhij,bhjv->bhiv', scores, vc) + jnp.einsum('bhid,bhdv->bhiv', qc * q_dec, s)
        s = s * chunk_dec + jnp.einsum('bhjd,bhjv->bhdv', kc * k_dec, vc)
        return s, o

    s_fin, o = lax.scan(step, s0, (to_chunks(q), to_chunks(k), to_chunks(v)))
    return o.transpose(1, 2, 0, 3, 4).reshape(b, hh, n, dv), s_fin


def retention_mixer(h, hc, w_in, decay_f, decay_b, gn_g, w_out, need_ctx):
    b, n, _ = h.shape

    def project(t, rope):
        m = t.shape[1]
        q, k, v, g = jnp.split(t @ w_in, [RET_QK_WIDTH, 2 * RET_QK_WIDTH,
                                          2 * RET_QK_WIDTH + RET_V_WIDTH], axis=-1)
        q = q.reshape(b, m, RET_HEADS, RET_QK_DIM)
        k = k.reshape(b, m, RET_HEADS, RET_QK_DIM) * (RET_QK_DIM ** -0.5)
        if rope is not None:
            q = apply_rope(q, *rope)
            k = apply_rope(k, *rope)
        v = v.reshape(b, m, RET_HEADS, RET_V_DIM)
        return q.transpose(0, 2, 1, 3), k.transpose(0, 2, 1, 3), v.transpose(0, 2, 1, 3), g

    def gated_out(o, g):
        mu = jnp.mean(o, axis=-1, keepdims=True)
        var = jnp.mean(jnp.square(o - mu), axis=-1, keepdims=True)
        on = (o - mu) * lax.rsqrt(var + GN_EPS)
        m = o.shape[2]
        on = on.transpose(0, 2, 1, 3).reshape(b, m, RET_V_WIDTH) * gn_g.astype(jnp.float32)
        return (on.astype(g.dtype) * jax.nn.silu(g)) @ w_out

    log_f = jnp.log1p(-jnp.exp(decay_f.astype(jnp.float32)))
    log_b = jnp.log1p(-jnp.exp(decay_b.astype(jnp.float32)))
    flip = lambda a: jnp.flip(a, axis=2)
    q, k, v, g = project(h, axial_rope(n, RET_QK_DIM))
    qc, kc, vc, gc = project(hc, None)
    zero = jnp.zeros((b, RET_HEADS, RET_QK_DIM, RET_V_DIM), jnp.float32)
    oc_f, s_f = chunk_retention(qc, kc, vc, log_f, zero, True)
    oc_b, s_b = chunk_retention(flip(qc), flip(kc), flip(vc), log_b, zero, False)
    o_f, _ = chunk_retention(q, k, v, log_f, s_f, True)
    o_b, _ = chunk_retention(flip(q), flip(k), flip(v), log_b, s_b, False)
    y = gated_out(o_f + flip(o_b), g)
    yc = gated_out(oc_f + flip(oc_b), gc) if need_ctx else None
    return y, yc


def setup_inputs(seed: int = 0) -> dict:
    key = jax.random.key(seed)
    ks = jax.random.split(key, 24)
    f32 = jnp.float32

    def nrm(k, shape):
        return jax.random.normal(k, shape, f32)

    def w(k, shape, fan_in, gain=1.0):
        return nrm(k, shape) * (gain * fan_in ** -0.5)

    def g1(k, shape):
        return 1.0 + 0.1 * nrm(k, shape)

    decay_base = -(5.0 + jnp.arange(RET_HEADS, dtype=f32)) * math.log(2.0)
    return {
        "x": nrm(ks[0], (BATCH, SEQ, D_MODEL)),
        "c": nrm(ks[1], (BATCH, D_MODEL)),
        "ctx": nrm(ks[2], (BATCH, CTX_LEN, D_MODEL)),
        "c_ctx": nrm(ks[3], (D_MODEL,)),
        "norm_g": g1(ks[4], (DEPTH, D_MODEL)),
        "mod_w": w(ks[5], (DEPTH, D_MODEL, 3 * D_MODEL), D_MODEL, 0.5),
        "mod_b": 0.02 * nrm(ks[6], (DEPTH, 3 * D_MODEL)),
        "pool_w_in": w(ks[7], (N_POOL_LAYERS, D_MODEL, 2 * POOL_WIDTH), D_MODEL),
        "pool_w_grp": w(ks[8], (N_POOL_LAYERS, len(POOL_WINDOWS), POOL_GROUP, POOL_GROUP), POOL_GROUP),
        "pool_scale": g1(ks[9], (N_POOL_LAYERS, POOL_WIDTH)),
        "pool_w_out": w(ks[10], (N_POOL_LAYERS, POOL_WIDTH, D_MODEL), POOL_WIDTH),
        "attn_w_in": w(ks[11], (N_ATTN_LAYERS, D_MODEL, ATTN_IN), D_MODEL),
        "attn_q_norm": g1(ks[12], (N_ATTN_LAYERS, ATTN_HEAD_DIM)),
        "attn_k_norm": g1(ks[13], (N_ATTN_LAYERS, ATTN_HEAD_DIM)),
        "attn_w_out": w(ks[14], (N_ATTN_LAYERS, ATTN_WIDTH, D_MODEL), ATTN_WIDTH),
        "ret_w_in": w(ks[15], (N_RET_LAYERS, D_MODEL, RET_IN), D_MODEL),
        "ret_decay_fwd": decay_base + 0.1 * nrm(ks[16], (N_RET_LAYERS, RET_HEADS)),
        "ret_decay_bwd": decay_base + 0.1 * nrm(ks[17], (N_RET_LAYERS, RET_HEADS)),
        "ret_gn_g": g1(ks[18], (N_RET_LAYERS, RET_V_WIDTH)),
        "ret_w_out": w(ks[19], (N_RET_LAYERS, RET_V_WIDTH, D_MODEL), RET_V_WIDTH),
        "final_norm_g": g1(ks[20], (D_MODEL,)),
    }


def reference(x, c, ctx, c_ctx, norm_g, mod_w, mod_b, pool_w_in, pool_w_grp, pool_scale, pool_w_out,
              attn_w_in, attn_q_norm, attn_k_norm, attn_w_out, ret_w_in, ret_decay_fwd, ret_decay_bwd,
              ret_gn_g, ret_w_out, final_norm_g):
    xc = ctx
    for i in range(DEPTH):
        kind, j = i % N_MIXERS, i // N_MIXERS
        need_ctx = i < DEPTH - 1
        shift, scale, gate = jnp.split(jax.nn.silu(c) @ mod_w[i] + mod_b[i], 3, axis=-1)
        h = modulate(rms_norm(x, norm_g[i]), shift[:, None, :], scale[:, None, :])
        if need_ctx or kind != 0:
            shift_c, scale_c, gate_c = jnp.split(jax.nn.silu(c_ctx) @ mod_w[i] + mod_b[i], 3, axis=-1)
            hc = modulate(rms_norm(xc, norm_g[i]), shift_c, scale_c)
        if kind == 0:
            y = pool_branch(h, pool_w_in[j], pool_w_grp[j], pool_scale[j], pool_w_out[j])
            yc = pool_branch(hc, pool_w_in[j], pool_w_grp[j], pool_scale[j], pool_w_out[j]) if need_ctx else None
        elif kind == 1:
            y, yc = attention_mixer(h, hc, attn_w_in[j], attn_q_norm[j], attn_k_norm[j], attn_w_out[j], need_ctx)
        else:
            y, yc = retention_mixer(h, hc, ret_w_in[j], ret_decay_fwd[j], ret_decay_bwd[j],
                                    ret_gn_g[j], ret_w_out[j], need_ctx)
        x = x + gate[:, None, :] * y.astype(x.dtype)
        if need_ctx:
            xc = xc + gate_c * yc.astype(xc.dtype)
    return rms_norm(x, final_norm_g)
```

```python
import functools
import math

import jax
import jax.numpy as jnp
from jax import lax
from jax.experimental import pallas as pl
from jax.experimental.pallas import tpu as pltpu

F32 = jnp.float32
BF16 = jnp.bfloat16

GRID_W = 64
ATTN_HEAD_DIM = 128
POOL_WINDOWS = (2, 4, 8, 16)
ROPE_THETA = 10000.0
NORM_EPS = 1e-6
GN_EPS = 1e-5

ROW_TILE = 256
HALO_ROWS = 16
MOD_ROWS = 8
V7X_VMEM_BYTES = 64 * 1024 * 1024
VMEM_CAP_BYTES = V7X_VMEM_BYTES - 8 * 1024 * 1024


def _vmem_limit(block_bytes, temp_bytes=0):
    need = 2 * block_bytes + temp_bytes + 4 * 1024 * 1024
    return int(min(max(need, 16 * 1024 * 1024), VMEM_CAP_BYTES))


def _params(semantics, block_bytes, temp_bytes=0):
    return pltpu.CompilerParams(dimension_semantics=semantics,
                                vmem_limit_bytes=_vmem_limit(block_bytes, temp_bytes))


def _largest_tile(n, candidates):
    for c in candidates:
        if n % c == 0:
            return c
    raise ValueError(f"no tile among {candidates} divides {n}")


def _silu(v):
    return v * (1.0 / (1.0 + jnp.exp(-v)))


def _mod_kernel(c_ref, w_ref, b_ref, o_ref):
    a = _silu(c_ref[...]).astype(BF16)
    w = w_ref[0].astype(BF16)
    o_ref[0] = jnp.dot(a, w, preferred_element_type=F32) + b_ref[0]


def _mod_tables(cvec, mod_w, mod_b):
    depth, d, n = mod_w.shape
    tn = _largest_tile(n, (512, 256, 128))
    blocks = MOD_ROWS * d * 4 + d * tn * 4 + tn * 4 + MOD_ROWS * tn * 4
    return pl.pallas_call(
        _mod_kernel,
        grid=(depth, n // tn),
        in_specs=[pl.BlockSpec((MOD_ROWS, d), lambda l, j: (0, 0)),
                  pl.BlockSpec((1, d, tn), lambda l, j: (l, 0, j)),
                  pl.BlockSpec((1, 1, tn), lambda l, j: (l, 0, j))],
        out_specs=pl.BlockSpec((1, MOD_ROWS, tn), lambda l, j: (l, 0, j)),
        out_shape=jax.ShapeDtypeStruct((depth, MOD_ROWS, n), F32),
        compiler_params=_params(("arbitrary", "arbitrary"), blocks, d * tn * 2),
        name="adaln_tables",
    )(cvec, mod_w, mod_b.reshape(depth, 1, n))


def _norm_mod_kernel(x_ref, g_ref, mod_ref, o_ref, *, d, ctx_tiles, ctx_row):
    b, t = pl.program_id(0), pl.program_id(1)
    row = jnp.where(t < ctx_tiles, ctx_row, b)
    shift = mod_ref[pl.ds(row, 1), 0:d]
    scale = mod_ref[pl.ds(row, 1), d:2 * d]
    x = x_ref[...]
    y = x * lax.rsqrt(jnp.mean(x * x, axis=-1, keepdims=True) + NORM_EPS) * g_ref[...]
    o_ref[...] = (y * (1.0 + scale) + shift).astype(o_ref.dtype)


def _norm_mod(xs, g, mods, *, batch, tiles, ctx_tiles):
    r, d = xs.shape
    blocks = ROW_TILE * d * 4 + d * 4 + MOD_ROWS * 3 * d * 4 + ROW_TILE * d * 2
    return pl.pallas_call(
        functools.partial(_norm_mod_kernel, d=d, ctx_tiles=ctx_tiles, ctx_row=batch),
        grid=(batch, tiles),
        in_specs=[pl.BlockSpec((ROW_TILE, d), lambda b, t: (b * tiles + t, 0)),
                  pl.BlockSpec((1, d), lambda b, t: (0, 0)),
                  pl.BlockSpec((MOD_ROWS, 3 * d), lambda b, t: (0, 0))],
        out_specs=pl.BlockSpec((ROW_TILE, d), lambda b, t: (b * tiles + t, 0)),
        out_shape=jax.ShapeDtypeStruct((r, d), BF16),
        compiler_params=_params(("arbitrary", "arbitrary"), blocks, 3 * ROW_TILE * d * 4),
        name="norm_modulate",
    )(xs, g.reshape(1, d), mods)


def _matmul_kernel(a_ref, b_ref, o_ref):
    o_ref[...] = jnp.dot(a_ref[...], b_ref[...], preferred_element_type=F32).astype(o_ref.dtype)


def _matmul(a, w, *, tm, out_dtype=BF16):
    m, k = a.shape
    n = w.shape[1]
    tn = _largest_tile(n, (1024, 512, 256, 128))
    blocks = tm * k * 2 + k * tn * 2 + tm * tn * jnp.dtype(out_dtype).itemsize
    return pl.pallas_call(
        _matmul_kernel,
        grid=(m // tm, n // tn),
        in_specs=[pl.BlockSpec((tm, k), lambda i, j: (i, 0)),
                  pl.BlockSpec((k, tn), lambda i, j: (0, j))],
        out_specs=pl.BlockSpec((tm, tn), lambda i, j: (i, j)),
        out_shape=jax.ShapeDtypeStruct((m, n), out_dtype),
        compiler_params=_params(("arbitrary", "arbitrary"), blocks, tm * tn * 4),
        name="in_proj",
    )(a, w)


def _matmul_res_kernel(a_ref, b_ref, x_ref, gate_ref, o_ref, acc_ref, *, nk, tm, ctx_rows, ctx_row):
    b, t, k = pl.program_id(0), pl.program_id(1), pl.program_id(3)
    part = jnp.dot(a_ref[...], b_ref[...], preferred_element_type=F32)

    @pl.when(k == 0)
    def _():
        acc_ref[...] = part

    @pl.when(k > 0)
    def _():
        acc_ref[...] += part

    @pl.when(k == nk - 1)
    def _():
        gate_b = gate_ref[pl.ds(b, 1), :]
        gate_c = gate_ref[pl.ds(ctx_row, 1), :]
        rows = t * tm + lax.broadcasted_iota(jnp.int32, (tm, 1), 0)
        gate = jnp.where(rows < ctx_rows, gate_c, gate_b)
        o_ref[...] = x_ref[...] + gate * acc_ref[...]


def _matmul_residual(z, w, xs, mods, *, batch, tm, ctx_rows):
    m, kdim = z.shape
    d = w.shape[1]
    tn = _largest_tile(d, (1024, 512, 256, 128))
    tk = _largest_tile(kdim, (2048, 1024, 512, 256, 128))
    nk = kdim // tk
    tiles = m // (batch * tm)
    gate_col0 = 2 * d // tn
    blocks = tm * tk * 2 + tk * tn * 2 + 2 * tm * tn * 4 + MOD_ROWS * tn * 4
    return pl.pallas_call(
        functools.partial(_matmul_res_kernel, nk=nk, tm=tm, ctx_rows=ctx_rows, ctx_row=batch),
        grid=(batch, tiles, d // tn, nk),
        in_specs=[pl.BlockSpec((tm, tk), lambda b, t, j, k: (b * tiles + t, k)),
                  pl.BlockSpec((tk, tn), lambda b, t, j, k: (k, j)),
                  pl.BlockSpec((tm, tn), lambda b, t, j, k: (b * tiles + t, j)),
                  pl.BlockSpec((MOD_ROWS, tn), lambda b, t, j, k: (0, gate_col0 + j))],
        out_specs=pl.BlockSpec((tm, tn), lambda b, t, j, k: (b * tiles + t, j)),
        out_shape=jax.ShapeDtypeStruct((m, d), F32),
        scratch_shapes=[pltpu.VMEM((tm, tn), F32)],
        input_output_aliases={2: 0},
        compiler_params=_params(("arbitrary",) * 4, blocks, 3 * tm * tn * 4),
        name="out_proj_residual",
    )(z, w, xs, mods)


def _pool_kernel(up_ref, u_ref, un_ref, g_ref, w_ref, s_ref, o_ref, e_ref, *,
                 ctx_tiles, tiles, ctx_rows, seq_rows):
    gi, t = pl.program_id(0), pl.program_id(2)
    tr = ROW_TILE
    in_ctx = t < ctx_tiles
    first = (t == 0) | (t == ctx_tiles)
    last = (t == ctx_tiles - 1) | (t == tiles - 1)
    seq_tile = jnp.where(in_ctx, t, t - ctx_tiles)
    n_seq = jnp.where(in_ctx, ctx_rows, seq_rows)

    u = u_ref[...].astype(F32)
    e_ref[HALO_ROWS:HALO_ROWS + tr, :] = u
    e_ref[0:HALO_ROWS, :] = jnp.where(first, 0.0, up_ref[...].astype(F32))
    e_ref[HALO_ROWS + tr:2 * HALO_ROWS + tr, :] = jnp.where(last, 0.0, un_ref[...].astype(F32))
    pos = seq_tile * tr + lax.broadcasted_iota(jnp.int32, (tr, 1), 0)

    for gidx, window in enumerate(POOL_WINDOWS):
        @pl.when(gi == gidx)
        def _(window=window):
            half = window // 2
            s = e_ref[pl.ds(HALO_ROWS - half, tr), :]
            for off in range(-half + 1, window - half):
                s = s + e_ref[pl.ds(HALO_ROWS + off, tr), :]
            cnt = jnp.minimum(pos + (window - half), n_seq) - jnp.maximum(pos - half, 0)
            pooled = s * (1.0 / cnt.astype(F32)) - u
            mixed = jnp.dot(pooled.astype(BF16), w_ref[0], preferred_element_type=F32)
            gate = _silu(g_ref[...].astype(F32))
            o_ref[...] = (mixed * s_ref[...] * gate).astype(o_ref.dtype)


def _pool_mix(ug, w_grp, scale, *, batch, tiles, ctx_tiles):
    r = ug.shape[0]
    n_groups, gdim, _ = w_grp.shape
    width = n_groups * gdim
    tr = ROW_TILE
    hb = tr // HALO_ROWS
    n_halo_blocks = r // HALO_ROWS
    kern = functools.partial(_pool_kernel, ctx_tiles=ctx_tiles, tiles=tiles,
                             ctx_rows=ctx_tiles * tr, seq_rows=(tiles - ctx_tiles) * tr)
    blocks = 3 * tr * gdim * 2 + 2 * HALO_ROWS * gdim * 2 + gdim * gdim * 2 + gdim * 4
    return pl.pallas_call(
        kern,
        grid=(n_groups, batch, tiles),
        in_specs=[pl.BlockSpec((HALO_ROWS, gdim),
                               lambda g, b, t: (jnp.maximum((b * tiles + t) * hb - 1, 0), g)),
                  pl.BlockSpec((tr, gdim), lambda g, b, t: (b * tiles + t, g)),
                  pl.BlockSpec((HALO_ROWS, gdim),
                               lambda g, b, t: (jnp.minimum((b * tiles + t + 1) * hb, n_halo_blocks - 1), g)),
                  pl.BlockSpec((tr, gdim), lambda g, b, t: (b * tiles + t, n_groups + g)),
                  pl.BlockSpec((1, gdim, gdim), lambda g, b, t: (g, 0, 0)),
                  pl.BlockSpec((1, gdim), lambda g, b, t: (0, g))],
        out_specs=pl.BlockSpec((tr, gdim), lambda g, b, t: (b * tiles + t, g)),
        out_shape=jax.ShapeDtypeStruct((r, width), BF16),
        scratch_shapes=[pltpu.VMEM((tr + 2 * HALO_ROWS, gdim), F32)],
        compiler_params=_params(("arbitrary",) * 3, blocks, 8 * tr * gdim * 4),
        name="pool_mix",
    )(ug, ug, ug, ug, w_grp, scale.reshape(1, width))


def _rope_tables(n_ctx, n_seq, head_dim):
    rows = n_seq // GRID_W
    row = jnp.repeat(jnp.arange(rows, dtype=F32), GRID_W)
    col = jnp.tile(jnp.arange(GRID_W, dtype=F32), rows)
    n_freq = head_dim // 4
    inv = ROPE_THETA ** (-jnp.arange(n_freq, dtype=F32) / n_freq)
    ang = jnp.concatenate([row[:, None] * inv, col[:, None] * inv], axis=-1)
    ang = jnp.concatenate([jnp.zeros((n_ctx, head_dim // 2), F32), ang], axis=0)
    cos = jnp.repeat(jnp.cos(ang), 2, axis=-1)
    sin = jnp.repeat(jnp.sin(ang), 2, axis=-1)
    sign = jnp.tile(jnp.array([-1.0, 1.0], F32), head_dim // 2)
    return cos, sin * sign


def _rope(x, cos, sin_signed):
    width = x.shape[-1]
    nxt = pltpu.roll(x, width - 1, axis=1)
    prv = pltpu.roll(x, 1, axis=1)
    lane = lax.broadcasted_iota(jnp.int32, x.shape, 1)
    partner = jnp.where(lane % 2 == 0, nxt, prv)
    return x * cos + partner * sin_signed


def _qk_prep_kernel(x_ref, cos_ref, sin_ref, qg_ref, kg_ref, o_ref, *, q_blocks, heads_per_block, scale):
    j = pl.program_id(2)
    is_q = j < q_blocks
    g = jnp.where(is_q, qg_ref[...], kg_ref[...])
    out_scale = jnp.where(is_q, scale, 1.0)
    cos, sin = cos_ref[...], sin_ref[...]
    x = x_ref[...].astype(F32)
    hd = ATTN_HEAD_DIM
    outs = []
    for h in range(heads_per_block):
        xh = x[:, h * hd:(h + 1) * hd]
        y = xh * lax.rsqrt(jnp.mean(xh * xh, axis=-1, keepdims=True) + NORM_EPS) * g
        outs.append(_rope(y, cos, sin) * out_scale)
    o_ref[...] = jnp.concatenate(outs, axis=1).astype(o_ref.dtype)


def _qk_prep(proj, cos, sin, q_g, k_g, *, batch, tiles, d, kv_width):
    r = proj.shape[0]
    hd = ATTN_HEAD_DIM
    heads_per_block = 4
    bw = heads_per_block * hd
    width = d + kv_width
    kern = functools.partial(_qk_prep_kernel, q_blocks=d // bw, heads_per_block=heads_per_block,
                             scale=hd ** -0.5)
    blocks = 2 * ROW_TILE * bw * 2 + 2 * ROW_TILE * hd * 4 + 2 * hd * 4
    return pl.pallas_call(
        kern,
        grid=(batch, tiles, width // bw),
        in_specs=[pl.BlockSpec((ROW_TILE, bw), lambda b, t, j: (b * tiles + t, j)),
                  pl.BlockSpec((ROW_TILE, hd), lambda b, t, j: (t, 0)),
                  pl.BlockSpec((ROW_TILE, hd), lambda b, t, j: (t, 0)),
                  pl.BlockSpec((1, hd), lambda b, t, j: (0, 0)),
                  pl.BlockSpec((1, hd), lambda b, t, j: (0, 0))],
        out_specs=pl.BlockSpec((ROW_TILE, bw), lambda b, t, j: (b * tiles + t, j)),
        out_shape=jax.ShapeDtypeStruct((r, width), BF16),
        compiler_params=_params(("arbitrary",) * 3, blocks, 8 * ROW_TILE * bw * 4),
        name="qk_norm_rope",
    )(proj, cos, sin, q_g.reshape(1, hd), k_g.reshape(1, hd))


def _attn_kernel(q_ref, k_ref, v_ref, g_ref, o_ref, m_ref, l_ref, acc_ref, *,
                 groups, ctx_tiles, ctx_rows, kv_rows, kv_chunk):
    t = pl.program_id(2)
    hd = ATTN_HEAD_DIM
    tq = ROW_TILE
    q = jnp.concatenate([q_ref[:, h * hd:(h + 1) * hd] for h in range(groups)], axis=0)

    m_ref[...] = jnp.full(m_ref.shape, -jnp.inf, F32)
    l_ref[...] = jnp.zeros(l_ref.shape, F32)
    acc_ref[...] = jnp.zeros(acc_ref.shape, F32)

    def step(start, size):
        k = k_ref[pl.ds(start, size), :]
        v = v_ref[pl.ds(start, size), :]
        s = lax.dot_general(q, k, (((1,), (1,)), ((), ())), preferred_element_type=F32)
        m_prev = m_ref[...]
        m_new = jnp.maximum(m_prev, jnp.max(s, axis=-1, keepdims=True))
        alpha = jnp.exp(m_prev - m_new)
        p = jnp.exp(s - m_new)
        l_ref[...] = alpha * l_ref[...] + jnp.sum(p, axis=-1, keepdims=True)
        acc_ref[...] = alpha * acc_ref[...] + jnp.dot(p.astype(BF16), v, preferred_element_type=F32)
        m_ref[...] = m_new

    @pl.when(t < ctx_tiles)
    def _():
        for c in range(ctx_rows // tq):
            step(c * tq, tq)

    @pl.when(t >= ctx_tiles)
    def _():
        def body(c, carry):
            step(pl.multiple_of(c * kv_chunk, kv_chunk), kv_chunk)
            return carry
        lax.fori_loop(0, kv_rows // kv_chunk, body, 0)

    o = acc_ref[...] * (1.0 / l_ref[...])
    o = jnp.concatenate([o[h * tq:(h + 1) * tq, :] for h in range(groups)], axis=1)
    o_ref[...] = (o * _silu(g_ref[...].astype(F32))).astype(o_ref.dtype)


def _attention(qk, proj, *, batch, tiles, ctx_tiles, d, kv_width):
    r = qk.shape[0]
    hd = ATTN_HEAD_DIM
    kv_heads = kv_width // hd
    groups = d // kv_width
    bw = groups * hd
    kv_rows = tiles * ROW_TILE
    kv_chunk = _largest_tile(kv_rows, (768, 512, 256))
    kern = functools.partial(_attn_kernel, groups=groups, ctx_tiles=ctx_tiles,
                             ctx_rows=ctx_tiles * ROW_TILE, kv_rows=kv_rows, kv_chunk=kv_chunk)
    blocks = 3 * ROW_TILE * bw * 2 + 2 * kv_rows * hd * 2
    temps = 4 * groups * ROW_TILE * kv_chunk * 4
    return pl.pallas_call(
        kern,
        grid=(batch, kv_heads, tiles),
        in_specs=[pl.BlockSpec((ROW_TILE, bw), lambda b, h, t: (b * tiles + t, h)),
                  pl.BlockSpec((kv_rows, hd), lambda b, h, t: (b, d // hd + h)),
                  pl.BlockSpec((kv_rows, hd), lambda b, h, t: (b, (d + kv_width) // hd + h)),
                  pl.BlockSpec((ROW_TILE, bw), lambda b, h, t: (b * tiles + t, (d + 2 * kv_width) // bw + h))],
        out_specs=pl.BlockSpec((ROW_TILE, bw), lambda b, h, t: (b * tiles + t, h)),
        out_shape=jax.ShapeDtypeStruct((r, d), BF16),
        scratch_shapes=[pltpu.VMEM((groups * ROW_TILE, 1), F32),
                        pltpu.VMEM((groups * ROW_TILE, 1), F32),
                        pltpu.VMEM((groups * ROW_TILE, hd), F32)],
        compiler_params=_params(("arbitrary",) * 3, blocks, temps),
        name="attention",
    )(qk, qk, proj, proj)


def _ret_common(q_ref, k_ref, cos_ref, sin_ref, dec_ref, qk_dim):
    cos, sin = cos_ref[...], sin_ref[...]
    q = _rope(q_ref[...].astype(F32), cos, sin)
    k = _rope(k_ref[...].astype(F32), cos, sin) * (qk_dim ** -0.5)
    log_g = jnp.log1p(-jnp.exp(dec_ref[0]))[0:1, 0:1]
    return q, k, log_g


def _ret_fwd_kernel(q_ref, k_ref, v_ref, cos_ref, sin_ref, dec_ref, o_ref, s_ref, *, qk_dim):
    c = pl.program_id(2)
    tc = ROW_TILE

    @pl.when(c == 0)
    def _():
        s_ref[...] = jnp.zeros(s_ref.shape, F32)

    q, k, log_g = _ret_common(q_ref, k_ref, cos_ref, sin_ref, dec_ref, qk_dim)
    ii = lax.broadcasted_iota(jnp.int32, (tc, tc), 0)
    jj = lax.broadcasted_iota(jnp.int32, (tc, tc), 1)
    diff = (ii - jj).astype(F32)
    decay = jnp.where(ii >= jj, jnp.exp(jnp.maximum(diff, 0.0) * log_g), 0.0)
    pos = lax.broadcasted_iota(jnp.int32, (tc, 1), 0).astype(F32)
    q_dec = jnp.exp((pos + 1.0) * log_g)
    k_dec = jnp.exp((tc - 1.0 - pos) * log_g)
    chunk_dec = jnp.exp(tc * log_g)

    v = v_ref[...]
    state = s_ref[...]
    scores = lax.dot_general(q.astype(BF16), k.astype(BF16), (((1,), (1,)), ((), ())),
                             preferred_element_type=F32) * decay
    o = jnp.dot(scores.astype(BF16), v, preferred_element_type=F32)
    o = o + jnp.dot((q * q_dec).astype(BF16), state.astype(BF16), preferred_element_type=F32)
    kt = (k * k_dec).T.astype(BF16)
    s_ref[...] = state * chunk_dec + jnp.dot(kt, v, preferred_element_type=F32)
    o_ref[...] = o.astype(o_ref.dtype)


def _ret_bwd_kernel(q_ref, k_ref, v_ref, cos_ref, sin_ref, dec_ref, of_ref, g_ref, gn_ref, o_ref, s_ref,
                    *, qk_dim):
    c = pl.program_id(2)
    tc = ROW_TILE

    @pl.when(c == 0)
    def _():
        s_ref[...] = jnp.zeros(s_ref.shape, F32)

    q, k, log_g = _ret_common(q_ref, k_ref, cos_ref, sin_ref, dec_ref, qk_dim)
    ii = lax.broadcasted_iota(jnp.int32, (tc, tc), 0)
    jj = lax.broadcasted_iota(jnp.int32, (tc, tc), 1)
    diff = (jj - ii).astype(F32)
    decay = jnp.where(jj > ii, jnp.exp(jnp.maximum(diff, 0.0) * log_g), 0.0)
    pos = lax.broadcasted_iota(jnp.int32, (tc, 1), 0).astype(F32)
    q_dec = jnp.exp((tc - pos) * log_g)
    k_dec = jnp.exp(pos * log_g)
    chunk_dec = jnp.exp(tc * log_g)

    v = v_ref[...]
    state = s_ref[...]
    scores = lax.dot_general(q.astype(BF16), k.astype(BF16), (((1,), (1,)), ((), ())),
                             preferred_element_type=F32) * decay
    o = jnp.dot(scores.astype(BF16), v, preferred_element_type=F32)
    o = o + jnp.dot((q * q_dec).astype(BF16), state.astype(BF16), preferred_element_type=F32)
    kt = (k * k_dec).T.astype(BF16)
    s_ref[...] = state * chunk_dec + jnp.dot(kt, v, preferred_element_type=F32)

    o = o + of_ref[...].astype(F32)
    mu = jnp.mean(o, axis=-1, keepdims=True)
    dev = o - mu
    var = jnp.mean(dev * dev, axis=-1, keepdims=True)
    on = dev * lax.rsqrt(var + GN_EPS) * gn_ref[...]
    o_ref[...] = (on * _silu(g_ref[...].astype(F32))).astype(o_ref.dtype)


def _retention(proj, cos, sin, decay_f, decay_b, gn_g, *, batch, tiles, ctx_tiles, d, heads):
    r = proj.shape[0]
    qk_dim = d // heads
    v_dim = 2 * qk_dim
    tc = ROW_TILE
    dec_f = jnp.broadcast_to(decay_f.astype(F32)[:, None, None], (heads, 8, 128))
    dec_b = jnp.broadcast_to(decay_b.astype(F32)[:, None, None], (heads, 8, 128))
    k_col0 = d // qk_dim
    v_col0 = 2 * d // v_dim
    g_col0 = (2 * d + heads * v_dim) // v_dim
    blocks = (2 * tc * qk_dim * 2 + 3 * tc * v_dim * 2 + 2 * tc * qk_dim * 4 + 8 * 128 * 4
              + tc * v_dim * 2 + v_dim * 4)
    temps = qk_dim * v_dim * 4 + 12 * tc * v_dim * 4

    def fwd_row(b, h, c):
        return b * tiles + c

    def bwd_chunk(c):
        return jnp.where(c < ctx_tiles, ctx_tiles - 1 - c, tiles - 1 - (c - ctx_tiles))

    def bwd_row(b, h, c):
        return b * tiles + bwd_chunk(c)

    o_f = pl.pallas_call(
        functools.partial(_ret_fwd_kernel, qk_dim=qk_dim),
        grid=(batch, heads, tiles),
        in_specs=[pl.BlockSpec((tc, qk_dim), lambda b, h, c: (fwd_row(b, h, c), h)),
                  pl.BlockSpec((tc, qk_dim), lambda b, h, c: (fwd_row(b, h, c), k_col0 + h)),
                  pl.BlockSpec((tc, v_dim), lambda b, h, c: (fwd_row(b, h, c), v_col0 + h)),
                  pl.BlockSpec((tc, qk_dim), lambda b, h, c: (c, 0)),
                  pl.BlockSpec((tc, qk_dim), lambda b, h, c: (c, 0)),
                  pl.BlockSpec((1, 8, 128), lambda b, h, c: (h, 0, 0))],
        out_specs=pl.BlockSpec((tc, v_dim), lambda b, h, c: (fwd_row(b, h, c), h)),
        out_shape=jax.ShapeDtypeStruct((r, heads * v_dim), BF16),
        scratch_shapes=[pltpu.VMEM((qk_dim, v_dim), F32)],
        compiler_params=_params(("arbitrary",) * 3, blocks, temps),
        name="retention_fwd",
    )(proj, proj, proj, cos, sin, dec_f)

    return pl.pallas_call(
        functools.partial(_ret_bwd_kernel, qk_dim=qk_dim),
        grid=(batch, heads, tiles),
        in_specs=[pl.BlockSpec((tc, qk_dim), lambda b, h, c: (bwd_row(b, h, c), h)),
                  pl.BlockSpec((tc, qk_dim), lambda b, h, c: (bwd_row(b, h, c), k_col0 + h)),
                  pl.BlockSpec((tc, v_dim), lambda b, h, c: (bwd_row(b, h, c), v_col0 + h)),
                  pl.BlockSpec((tc, qk_dim), lambda b, h, c: (bwd_chunk(c), 0)),
                  pl.BlockSpec((tc, qk_dim), lambda b, h, c: (bwd_chunk(c), 0)),
                  pl.BlockSpec((1, 8, 128), lambda b, h, c: (h, 0, 0)),
                  pl.BlockSpec((tc, v_dim), lambda b, h, c: (bwd_row(b, h, c), h)),
                  pl.BlockSpec((tc, v_dim), lambda b, h, c: (bwd_row(b, h, c), g_col0 + h)),
                  pl.BlockSpec((1, v_dim), lambda b, h, c: (0, h))],
        out_specs=pl.BlockSpec((tc, v_dim), lambda b, h, c: (bwd_row(b, h, c), h)),
        out_shape=jax.ShapeDtypeStruct((r, heads * v_dim), BF16),
        scratch_shapes=[pltpu.VMEM((qk_dim, v_dim), F32)],
        compiler_params=_params(("arbitrary",) * 3, blocks, temps),
        name="retention_bwd_norm_gate",
    )(proj, proj, proj, cos, sin, dec_b, o_f, proj, gn_g.astype(F32).reshape(1, heads * v_dim))


def _final_norm_kernel(x_ref, g_ref, o_ref):
    x = x_ref[...]
    o_ref[...] = x * lax.rsqrt(jnp.mean(x * x, axis=-1, keepdims=True) + NORM_EPS) * g_ref[...]


def _final_norm(xs, g, *, batch, tiles, ctx_tiles):
    d = xs.shape[1]
    seq_tiles = tiles - ctx_tiles
    blocks = 2 * ROW_TILE * d * 4 + d * 4
    return pl.pallas_call(
        _final_norm_kernel,
        grid=(batch, seq_tiles),
        in_specs=[pl.BlockSpec((ROW_TILE, d), lambda b, t: (b * tiles + ctx_tiles + t, 0)),
                  pl.BlockSpec((1, d), lambda b, t: (0, 0))],
        out_specs=pl.BlockSpec((ROW_TILE, d), lambda b, t: (b * seq_tiles + t, 0)),
        out_shape=jax.ShapeDtypeStruct((batch * seq_tiles * ROW_TILE, d), F32),
        compiler_params=_params(("arbitrary", "arbitrary"), blocks, 2 * ROW_TILE * d * 4),
        name="final_norm",
    )(xs, g.reshape(1, d))


def kernel(x, c, ctx, c_ctx, norm_g, mod_w, mod_b, pool_w_in, pool_w_grp, pool_scale, pool_w_out,
           attn_w_in, attn_q_norm, attn_k_norm, attn_w_out, ret_w_in, ret_decay_fwd, ret_decay_bwd,
           ret_gn_g, ret_w_out, final_norm_g):
    batch, seq, d = x.shape
    n_ctx = ctx.shape[1]
    depth = norm_g.shape[0]
    assert seq % ROW_TILE == 0 and n_ctx % ROW_TILE == 0 and seq % GRID_W == 0
    assert batch < MOD_ROWS
    ctx_tiles = n_ctx // ROW_TILE
    tiles = (n_ctx + seq) // ROW_TILE
    rows_per_sample = n_ctx + seq
    tm = _largest_tile(rows_per_sample, (768, 512, 256))

    xs = jnp.concatenate([ctx, x], axis=1).reshape(batch * rows_per_sample, d)
    cvec = jnp.concatenate([c, c_ctx[None, :], jnp.zeros((MOD_ROWS - batch - 1, d), F32)], axis=0)
    mods = _mod_tables(cvec, mod_w, mod_b)
    geom = dict(batch=batch, tiles=tiles, ctx_tiles=ctx_tiles)

    kv_width = (attn_w_in.shape[2] - 2 * d) // 2
    ret_heads = ret_decay_fwd.shape[1]
    attn_cos, attn_sin = _rope_tables(n_ctx, seq, ATTN_HEAD_DIM)
    ret_cos, ret_sin = _rope_tables(n_ctx, seq, d // ret_heads)

    for i in range(depth):
        kind, j = i % 3, i // 3
        h = _norm_mod(xs, norm_g[i], mods[i], **geom)
        if kind == 0:
            ug = _matmul(h, pool_w_in[j].astype(BF16), tm=tm)
            z = _pool_mix(ug, pool_w_grp[j].astype(BF16), pool_scale[j], **geom)
            w_out = pool_w_out[j]
        elif kind == 1:
            proj = _matmul(h, attn_w_in[j].astype(BF16), tm=tm)
            qk = _qk_prep(proj, attn_cos, attn_sin, attn_q_norm[j], attn_k_norm[j],
                          batch=batch, tiles=tiles, d=d, kv_width=kv_width)
            z = _attention(qk, proj, **geom, d=d, kv_width=kv_width)
            w_out = attn_w_out[j]
        else:
            proj = _matmul(h, ret_w_in[j].astype(BF16), tm=tm)
            z = _retention(proj, ret_cos, ret_sin, ret_decay_fwd[j], ret_decay_bwd[j], ret_gn_g[j],
                           **geom, d=d, heads=ret_heads)
            w_out = ret_w_out[j]
        xs = _matmul_residual(z, w_out.astype(BF16), xs, mods[i], batch=batch, tm=tm, ctx_rows=n_ctx)

    out = _final_norm(xs, final_norm_g, **geom)
    return out.reshape(batch, seq, d)
```

```python
import functools
import math

import jax
import jax.numpy as jnp
from jax import lax
from jax.experimental import pallas as pl
from jax.experimental.pallas import tpu as pltpu

F32 = jnp.float32
BF16 = jnp.bfloat16

GRID_W = 64
ATTN_HEAD_DIM = 128
POOL_WINDOWS = (2, 4, 8, 16)
ROPE_THETA = 10000.0
NORM_EPS = 1e-6
GN_EPS = 1e-5

ROW_TILE = 256
HALO_ROWS = 16
EDGE = max(POOL_WINDOWS) // 2
MOD_ROWS = 8
V7X_VMEM_BYTES = 64 * 1024 * 1024
VMEM_CAP_BYTES = V7X_VMEM_BYTES - 8 * 1024 * 1024


def _vmem_limit(block_bytes, temp_bytes=0):
    need = 2 * block_bytes + temp_bytes + 4 * 1024 * 1024
    return int(min(max(need, 16 * 1024 * 1024), VMEM_CAP_BYTES))


def _params(semantics, block_bytes, temp_bytes=0):
    return pltpu.CompilerParams(dimension_semantics=semantics,
                                vmem_limit_bytes=_vmem_limit(block_bytes, temp_bytes))


def _largest_tile(n, candidates):
    for c in candidates:
        if n % c == 0:
            return c
    raise ValueError(f"no tile among {candidates} divides {n}")


def _silu(v):
    return v * (1.0 / (1.0 + jnp.exp(-v)))


def _mod_kernel(c_ref, w_ref, b_ref, o_ref):
    a = _silu(c_ref[...]).astype(BF16)
    w = w_ref[0].astype(BF16)
    o_ref[0] = jnp.dot(a, w, preferred_element_type=F32) + b_ref[0]


def _mod_tables(cvec, mod_w, mod_b):
    depth, d, n = mod_w.shape
    tn = _largest_tile(n, (512, 256, 128))
    blocks = MOD_ROWS * d * 4 + d * tn * 4 + tn * 4 + MOD_ROWS * tn * 4
    return pl.pallas_call(
        _mod_kernel,
        grid=(depth, n // tn),
        in_specs=[pl.BlockSpec((MOD_ROWS, d), lambda l, j: (0, 0)),
                  pl.BlockSpec((1, d, tn), lambda l, j: (l, 0, j)),
                  pl.BlockSpec((1, 1, tn), lambda l, j: (l, 0, j))],
        out_specs=pl.BlockSpec((1, MOD_ROWS, tn), lambda l, j: (l, 0, j)),
        out_shape=jax.ShapeDtypeStruct((depth, MOD_ROWS, n), F32),
        compiler_params=_params(("arbitrary", "arbitrary"), blocks, d * tn * 2),
        name="adaln_tables",
    )(cvec, mod_w, mod_b.reshape(depth, 1, n))


def _norm_mod_kernel(x_ref, g_ref, mod_ref, o_ref, *, d, ctx_tiles, ctx_row):
    b, t = pl.program_id(0), pl.program_id(1)
    row = jnp.where(t < ctx_tiles, ctx_row, b)
    shift = mod_ref[pl.ds(row, 1), 0:d]
    scale = mod_ref[pl.ds(row, 1), d:2 * d]
    x = x_ref[...]
    y = x * lax.rsqrt(jnp.mean(x * x, axis=-1, keepdims=True) + NORM_EPS) * g_ref[...]
    o_ref[...] = (y * (1.0 + scale) + shift).astype(o_ref.dtype)


def _norm_mod(xs, g, mods, *, batch, tiles, ctx_tiles):
    r, d = xs.shape
    blocks = ROW_TILE * d * 4 + d * 4 + MOD_ROWS * 3 * d * 4 + ROW_TILE * d * 2
    return pl.pallas_call(
        functools.partial(_norm_mod_kernel, d=d, ctx_tiles=ctx_tiles, ctx_row=batch),
        grid=(batch, tiles),
        in_specs=[pl.BlockSpec((ROW_TILE, d), lambda b, t: (b * tiles + t, 0)),
                  pl.BlockSpec((1, d), lambda b, t: (0, 0)),
                  pl.BlockSpec((MOD_ROWS, 3 * d), lambda b, t: (0, 0))],
        out_specs=pl.BlockSpec((ROW_TILE, d), lambda b, t: (b * tiles + t, 0)),
        out_shape=jax.ShapeDtypeStruct((r, d), BF16),
        compiler_params=_params(("arbitrary", "arbitrary"), blocks, 3 * ROW_TILE * d * 4),
        name="norm_modulate",
    )(xs, g.reshape(1, d), mods)


def _matmul_kernel(a_ref, b_ref, o_ref):
    o_ref[...] = jnp.dot(a_ref[...], b_ref[...], preferred_element_type=F32).astype(o_ref.dtype)


def _matmul(a, w, *, tm, out_dtype=BF16):
    m, k = a.shape
    n = w.shape[1]
    tn = _largest_tile(n, (1024, 512, 256, 128))
    blocks = tm * k * 2 + k * tn * 2 + tm * tn * jnp.dtype(out_dtype).itemsize
    return pl.pallas_call(
        _matmul_kernel,
        grid=(m // tm, n // tn),
        in_specs=[pl.BlockSpec((tm, k), lambda i, j: (i, 0)),
                  pl.BlockSpec((k, tn), lambda i, j: (0, j))],
        out_specs=pl.BlockSpec((tm, tn), lambda i, j: (i, j)),
        out_shape=jax.ShapeDtypeStruct((m, n), out_dtype),
        compiler_params=_params(("arbitrary", "arbitrary"), blocks, tm * tn * 4),
        name="in_proj",
    )(a, w)


def _matmul_res_kernel(a_ref, b_ref, x_ref, gate_ref, o_ref, *, tm, ctx_rows, ctx_row):
    b, t = pl.program_id(0), pl.program_id(1)
    y = jnp.dot(a_ref[...], b_ref[...], preferred_element_type=F32)
    gate_b = gate_ref[pl.ds(b, 1), :]
    gate_c = gate_ref[pl.ds(ctx_row, 1), :]
    rows = t * tm + lax.broadcasted_iota(jnp.int32, (tm, 1), 0)
    gate = jnp.where(rows < ctx_rows, gate_c, gate_b)
    o_ref[...] = x_ref[...] + gate * y


def _matmul_residual(z, w, xs, mods, *, batch, tm, ctx_rows):
    m, kdim = z.shape
    d = w.shape[1]
    for tn in (1024, 512, 256, 128):
        blocks = tm * kdim * 2 + kdim * tn * 2 + 2 * tm * tn * 4 + MOD_ROWS * tn * 4
        if d % tn == 0 and 2 * blocks + 2 * tm * tn * 4 + 4 * 1024 * 1024 <= VMEM_CAP_BYTES:
            break
    else:
        raise ValueError("out-projection blocks do not fit VMEM")
    tiles = m // (batch * tm)
    gate_col0 = 2 * d // tn
    return pl.pallas_call(
        functools.partial(_matmul_res_kernel, tm=tm, ctx_rows=ctx_rows, ctx_row=batch),
        grid=(batch, tiles, d // tn),
        in_specs=[pl.BlockSpec((tm, kdim), lambda b, t, j: (b * tiles + t, 0)),
                  pl.BlockSpec((kdim, tn), lambda b, t, j: (0, j)),
                  pl.BlockSpec((tm, tn), lambda b, t, j: (b * tiles + t, j)),
                  pl.BlockSpec((MOD_ROWS, tn), lambda b, t, j: (0, gate_col0 + j))],
        out_specs=pl.BlockSpec((tm, tn), lambda b, t, j: (b * tiles + t, j)),
        out_shape=jax.ShapeDtypeStruct((m, d), F32),
        input_output_aliases={2: 0},
        compiler_params=_params(("arbitrary",) * 3, blocks, 2 * tm * tn * 4),
        name="out_proj_residual",
    )(z, w, xs, mods)


def _edge_window_sums(x, window):
    half = window // 2
    total = x[EDGE - half:2 * EDGE - half, :]
    for off in range(-half + 1, window - half):
        total = total + x[EDGE + off:2 * EDGE + off, :]
    return total


def _pool_kernel(up_ref, u_ref, un_ref, g_ref, w_ref, s_ref, o_ref, *,
                 ctx_tiles, tiles, ctx_rows, seq_rows):
    gi, t = pl.program_id(0), pl.program_id(2)
    tr = ROW_TILE
    in_ctx = t < ctx_tiles
    first = (t == 0) | (t == ctx_tiles)
    last = (t == ctx_tiles - 1) | (t == tiles - 1)
    seq_tile = jnp.where(in_ctx, t, t - ctx_tiles)
    n_seq = jnp.where(in_ctx, ctx_rows, seq_rows)

    ub = u_ref[...]
    u = ub.astype(F32)
    above = jnp.where(first, 0.0, up_ref[HALO_ROWS - EDGE:HALO_ROWS, :].astype(F32))
    below = jnp.where(last, 0.0, un_ref[0:EDGE, :].astype(F32))
    top = jnp.concatenate([above, u[0:2 * EDGE, :]], axis=0)
    bottom = jnp.concatenate([u[tr - 2 * EDGE:tr, :], below], axis=0)
    pos = seq_tile * tr + lax.broadcasted_iota(jnp.int32, (tr, 1), 0)
    delta = lax.broadcasted_iota(jnp.int32, (tr, tr), 1) - lax.broadcasted_iota(jnp.int32, (tr, tr), 0)

    for gidx, window in enumerate(POOL_WINDOWS):
        @pl.when(gi == gidx)
        def _(window=window):
            half = window // 2
            band = jnp.where((delta >= -half) & (delta < window - half), 1.0, 0.0).astype(BF16)
            s = jnp.dot(band, ub, preferred_element_type=F32)
            s = jnp.concatenate([_edge_window_sums(top, window), s[EDGE:tr - EDGE, :],
                                 _edge_window_sums(bottom, window)], axis=0)
            cnt = jnp.minimum(pos + (window - half), n_seq) - jnp.maximum(pos - half, 0)
            pooled = s * (1.0 / cnt.astype(F32)) - u
            mixed = jnp.dot(pooled.astype(BF16), w_ref[0], preferred_element_type=F32)
            gate = _silu(g_ref[...].astype(F32))
            o_ref[...] = (mixed * s_ref[...] * gate).astype(o_ref.dtype)


def _pool_mix(ug, w_grp, scale, *, batch, tiles, ctx_tiles):
    r = ug.shape[0]
    n_groups, gdim, _ = w_grp.shape
    width = n_groups * gdim
    tr = ROW_TILE
    hb = tr // HALO_ROWS
    n_halo_blocks = r // HALO_ROWS
    kern = functools.partial(_pool_kernel, ctx_tiles=ctx_tiles, tiles=tiles,
                             ctx_rows=ctx_tiles * tr, seq_rows=(tiles - ctx_tiles) * tr)
    blocks = 3 * tr * gdim * 2 + 2 * HALO_ROWS * gdim * 2 + gdim * gdim * 2 + gdim * 4
    return pl.pallas_call(
        kern,
        grid=(n_groups, batch, tiles),
        in_specs=[pl.BlockSpec((HALO_ROWS, gdim),
                               lambda g, b, t: (jnp.maximum((b * tiles + t) * hb - 1, 0), g)),
                  pl.BlockSpec((tr, gdim), lambda g, b, t: (b * tiles + t, g)),
                  pl.BlockSpec((HALO_ROWS, gdim),
                               lambda g, b, t: (jnp.minimum((b * tiles + t + 1) * hb, n_halo_blocks - 1), g)),
                  pl.BlockSpec((tr, gdim), lambda g, b, t: (b * tiles + t, n_groups + g)),
                  pl.BlockSpec((1, gdim, gdim), lambda g, b, t: (g, 0, 0)),
                  pl.BlockSpec((1, gdim), lambda g, b, t: (0, g))],
        out_specs=pl.BlockSpec((tr, gdim), lambda g, b, t: (b * tiles + t, g)),
        out_shape=jax.ShapeDtypeStruct((r, width), BF16),
        compiler_params=_params(("arbitrary",) * 3, blocks, 8 * tr * gdim * 4),
        name="pool_mix",
    )(ug, ug, ug, ug, w_grp, scale.reshape(1, width))


def _rope_tables(n_ctx, n_seq, head_dim):
    rows = n_seq // GRID_W
    row = jnp.repeat(jnp.arange(rows, dtype=F32), GRID_W)
    col = jnp.tile(jnp.arange(GRID_W, dtype=F32), rows)
    n_freq = head_dim // 4
    inv = ROPE_THETA ** (-jnp.arange(n_freq, dtype=F32) / n_freq)
    ang = jnp.concatenate([row[:, None] * inv, col[:, None] * inv], axis=-1)
    ang = jnp.concatenate([jnp.zeros((n_ctx, head_dim // 2), F32), ang], axis=0)
    cos = jnp.repeat(jnp.cos(ang), 2, axis=-1)
    sin = jnp.repeat(jnp.sin(ang), 2, axis=-1)
    sign = jnp.tile(jnp.array([-1.0, 1.0], F32), head_dim // 2)
    return cos, sin * sign


def _rope(x, cos, sin_signed):
    width = x.shape[-1]
    nxt = pltpu.roll(x, width - 1, axis=1)
    prv = pltpu.roll(x, 1, axis=1)
    lane = lax.broadcasted_iota(jnp.int32, x.shape, 1)
    partner = jnp.where(lane % 2 == 0, nxt, prv)
    return x * cos + partner * sin_signed


def _qk_prep_kernel(x_ref, cos_ref, sin_ref, qg_ref, kg_ref, o_ref, *, q_blocks, heads_per_block, scale):
    j = pl.program_id(2)
    is_q = j < q_blocks
    g = jnp.where(is_q, qg_ref[...], kg_ref[...])
    out_scale = jnp.where(is_q, scale, 1.0)
    cos, sin = cos_ref[...], sin_ref[...]
    x = x_ref[...].astype(F32)
    hd = ATTN_HEAD_DIM
    outs = []
    for h in range(heads_per_block):
        xh = x[:, h * hd:(h + 1) * hd]
        y = xh * lax.rsqrt(jnp.mean(xh * xh, axis=-1, keepdims=True) + NORM_EPS) * g
        outs.append(_rope(y, cos, sin) * out_scale)
    o_ref[...] = jnp.concatenate(outs, axis=1).astype(o_ref.dtype)


def _qk_prep(proj, cos, sin, q_g, k_g, *, batch, tiles, d, kv_width):
    r = proj.shape[0]
    hd = ATTN_HEAD_DIM
    heads_per_block = 4
    bw = heads_per_block * hd
    width = d + kv_width
    kern = functools.partial(_qk_prep_kernel, q_blocks=d // bw, heads_per_block=heads_per_block,
                             scale=hd ** -0.5 * math.log2(math.e))
    blocks = 2 * ROW_TILE * bw * 2 + 2 * ROW_TILE * hd * 4 + 2 * hd * 4
    return pl.pallas_call(
        kern,
        grid=(batch, tiles, width // bw),
        in_specs=[pl.BlockSpec((ROW_TILE, bw), lambda b, t, j: (b * tiles + t, j)),
                  pl.BlockSpec((ROW_TILE, hd), lambda b, t, j: (t, 0)),
                  pl.BlockSpec((ROW_TILE, hd), lambda b, t, j: (t, 0)),
                  pl.BlockSpec((1, hd), lambda b, t, j: (0, 0)),
                  pl.BlockSpec((1, hd), lambda b, t, j: (0, 0))],
        out_specs=pl.BlockSpec((ROW_TILE, bw), lambda b, t, j: (b * tiles + t, j)),
        out_shape=jax.ShapeDtypeStruct((r, width), BF16),
        compiler_params=_params(("arbitrary",) * 3, blocks, 8 * ROW_TILE * bw * 4),
        name="qk_norm_rope",
    )(proj, cos, sin, q_g.reshape(1, hd), k_g.reshape(1, hd))


def _v_transpose_kernel(v_ref, o_ref):
    o_ref[...] = v_ref[...].astype(F32).T.astype(o_ref.dtype)


def _v_transpose(proj, *, batch, n_chunks, kv_chunk, d, kv_width):
    hd = ATTN_HEAD_DIM
    kv_heads = kv_width // hd
    v_col0 = (d + kv_width) // hd
    return pl.pallas_call(
        _v_transpose_kernel,
        grid=(batch, kv_heads, n_chunks),
        in_specs=[pl.BlockSpec((kv_chunk, hd), lambda b, h, c: (b * n_chunks + c, v_col0 + h))],
        out_specs=pl.BlockSpec((None, None, None, hd, kv_chunk), lambda b, h, c: (b, h, c, 0, 0)),
        out_shape=jax.ShapeDtypeStruct((batch, kv_heads, n_chunks, hd, kv_chunk), BF16),
        compiler_params=_params(("arbitrary",) * 3, 2 * kv_chunk * hd * 2, 4 * kv_chunk * hd * 4),
        name="v_transpose",
    )(proj)


def _attn_kernel(q_ref, k_ref, vt_ref, g_ref, o_ref, s_ref, m_ref, l_ref, acc_ref, *,
                 groups, ctx_tiles, ctx_rows, n_chunks, kv_chunk):
    t = pl.program_id(2)
    hd = ATTN_HEAD_DIM
    nt = (((1,), (1,)), ((), ()))

    m_ref[...] = jnp.full(m_ref.shape, -jnp.inf, F32)
    l_ref[...] = jnp.zeros(l_ref.shape, F32)
    acc_ref[...] = jnp.zeros(acc_ref.shape, F32)

    def scores(k, slot, n):
        for h in range(groups):
            s_ref[slot, h, 0:n, :] = lax.dot_general(k, q_ref[:, h * hd:(h + 1) * hd], nt,
                                                     preferred_element_type=F32)

    def softmax_pv(vt, slot, n):
        weights = []
        for h in range(groups):
            s = s_ref[slot, h, 0:n, :]
            m_prev = m_ref[h]
            m_new = jnp.maximum(m_prev, jnp.max(s, axis=0, keepdims=True))
            alpha = jnp.exp2(m_prev - m_new)
            p = jnp.exp2(s - m_new)
            l_ref[h] = alpha * l_ref[h] + jnp.sum(p, axis=0, keepdims=True)
            m_ref[h] = m_new
            weights.append((alpha, p.astype(BF16)))
        for h, (alpha, p) in enumerate(weights):
            acc_ref[h] = alpha * acc_ref[h] + jnp.dot(vt, p, preferred_element_type=F32)

    def k_chunk(c):
        return k_ref[pl.ds(pl.multiple_of(c * kv_chunk, kv_chunk), kv_chunk), :]

    @pl.when(t < ctx_tiles)
    def _():
        for r0 in range(0, ctx_rows, ROW_TILE):
            c, off = divmod(r0, kv_chunk)
            scores(k_ref[r0:r0 + ROW_TILE, :], 0, ROW_TILE)
            softmax_pv(vt_ref[c, :, off:off + ROW_TILE], 0, ROW_TILE)

    @pl.when(t >= ctx_tiles)
    def _():
        scores(k_chunk(0), 0, kv_chunk)
        pairs = (n_chunks - 1) // 2

        def body(i, carry):
            c = 2 * i
            scores(k_chunk(c + 1), 1, kv_chunk)
            softmax_pv(vt_ref[c], 0, kv_chunk)
            scores(k_chunk(c + 2), 0, kv_chunk)
            softmax_pv(vt_ref[c + 1], 1, kv_chunk)
            return carry
        lax.fori_loop(0, pairs, body, 0)
        if n_chunks % 2 == 0:
            scores(k_chunk(n_chunks - 1), 1, kv_chunk)
        softmax_pv(vt_ref[2 * pairs], 0, kv_chunk)
        if n_chunks % 2 == 0:
            softmax_pv(vt_ref[n_chunks - 1], 1, kv_chunk)

    o = jnp.concatenate([(acc_ref[h] * (1.0 / l_ref[h])).T for h in range(groups)], axis=1)
    o_ref[...] = (o * _silu(g_ref[...].astype(F32))).astype(o_ref.dtype)


def _attention(qk, proj, *, batch, tiles, ctx_tiles, d, kv_width):
    r = qk.shape[0]
    hd = ATTN_HEAD_DIM
    kv_heads = kv_width // hd
    groups = d // kv_width
    bw = groups * hd
    kv_rows = tiles * ROW_TILE
    kv_chunk = _largest_tile(kv_rows, (768, 512, 256))
    n_chunks = kv_rows // kv_chunk
    vt = _v_transpose(proj, batch=batch, n_chunks=n_chunks, kv_chunk=kv_chunk, d=d, kv_width=kv_width)
    kern = functools.partial(_attn_kernel, groups=groups, ctx_tiles=ctx_tiles,
                             ctx_rows=ctx_tiles * ROW_TILE, n_chunks=n_chunks, kv_chunk=kv_chunk)
    blocks = 3 * ROW_TILE * bw * 2 + 2 * kv_rows * hd * 2
    temps = (2 * groups + 6) * kv_chunk * ROW_TILE * 4 + groups * (hd + 16) * ROW_TILE * 4
    return pl.pallas_call(
        kern,
        grid=(batch, kv_heads, tiles),
        in_specs=[pl.BlockSpec((ROW_TILE, bw), lambda b, h, t: (b * tiles + t, h)),
                  pl.BlockSpec((kv_rows, hd), lambda b, h, t: (b, d // hd + h)),
                  pl.BlockSpec((None, None, n_chunks, hd, kv_chunk), lambda b, h, t: (b, h, 0, 0, 0)),
                  pl.BlockSpec((ROW_TILE, bw), lambda b, h, t: (b * tiles + t, (d + 2 * kv_width) // bw + h))],
        out_specs=pl.BlockSpec((ROW_TILE, bw), lambda b, h, t: (b * tiles + t, h)),
        out_shape=jax.ShapeDtypeStruct((r, d), BF16),
        scratch_shapes=[pltpu.VMEM((2, groups, kv_chunk, ROW_TILE), F32),
                        pltpu.VMEM((groups, 1, ROW_TILE), F32),
                        pltpu.VMEM((groups, 1, ROW_TILE), F32),
                        pltpu.VMEM((groups, hd, ROW_TILE), F32)],
        compiler_params=_params(("arbitrary",) * 3, blocks, temps),
        name="attention",
    )(qk, qk, vt, proj)


def _ret_common(q_ref, k_ref, cos_ref, sin_ref, dec_ref, qk_dim):
    cos, sin = cos_ref[...], sin_ref[...]
    q = _rope(q_ref[...].astype(F32), cos, sin)
    k = _rope(k_ref[...].astype(F32), cos, sin) * (qk_dim ** -0.5)
    log_g = jnp.log1p(-jnp.exp(dec_ref[0]))[0:1, 0:1]
    return q, k, log_g


def _ret_fwd_kernel(q_ref, k_ref, v_ref, cos_ref, sin_ref, dec_ref, o_ref, s_ref, *, qk_dim):
    c = pl.program_id(2)
    tc = ROW_TILE

    @pl.when(c == 0)
    def _():
        s_ref[...] = jnp.zeros(s_ref.shape, F32)

    q, k, log_g = _ret_common(q_ref, k_ref, cos_ref, sin_ref, dec_ref, qk_dim)
    ii = lax.broadcasted_iota(jnp.int32, (tc, tc), 0)
    jj = lax.broadcasted_iota(jnp.int32, (tc, tc), 1)
    diff = (ii - jj).astype(F32)
    decay = jnp.where(ii >= jj, jnp.exp(jnp.maximum(diff, 0.0) * log_g), 0.0)
    pos = lax.broadcasted_iota(jnp.int32, (tc, 1), 0).astype(F32)
    q_dec = jnp.exp((pos + 1.0) * log_g)
    k_dec = jnp.exp((tc - 1.0 - pos) * log_g)
    chunk_dec = jnp.exp(tc * log_g)

    v = v_ref[...]
    state = s_ref[...]
    scores = lax.dot_general(q.astype(BF16), k.astype(BF16), (((1,), (1,)), ((), ())),
                             preferred_element_type=F32) * decay
    o = jnp.dot(scores.astype(BF16), v, preferred_element_type=F32)
    o = o + jnp.dot((q * q_dec).astype(BF16), state.astype(BF16), preferred_element_type=F32)
    kt = (k * k_dec).T.astype(BF16)
    s_ref[...] = state * chunk_dec + jnp.dot(kt, v, preferred_element_type=F32)
    o_ref[...] = o.astype(o_ref.dtype)


def _ret_bwd_kernel(q_ref, k_ref, v_ref, cos_ref, sin_ref, dec_ref, of_ref, g_ref, gn_ref, o_ref, s_ref,
                    *, qk_dim):
    c = pl.program_id(2)
    tc = ROW_TILE

    @pl.when(c == 0)
    def _():
        s_ref[...] = jnp.zeros(s_ref.shape, F32)

    q, k, log_g = _ret_common(q_ref, k_ref, cos_ref, sin_ref, dec_ref, qk_dim)
    ii = lax.broadcasted_iota(jnp.int32, (tc, tc), 0)
    jj = lax.broadcasted_iota(jnp.int32, (tc, tc), 1)
    diff = (jj - ii).astype(F32)
    decay = jnp.where(jj > ii, jnp.exp(jnp.maximum(diff, 0.0) * log_g), 0.0)
    pos = lax.broadcasted_iota(jnp.int32, (tc, 1), 0).astype(F32)
    q_dec = jnp.exp((tc - pos) * log_g)
    k_dec = jnp.exp(pos * log_g)
    chunk_dec = jnp.exp(tc * log_g)

    v = v_ref[...]
    state = s_ref[...]
    scores = lax.dot_general(q.astype(BF16), k.astype(BF16), (((1,), (1,)), ((), ())),
                             preferred_element_type=F32) * decay
    o = jnp.dot(scores.astype(BF16), v, preferred_element_type=F32)
    o = o + jnp.dot((q * q_dec).astype(BF16), state.astype(BF16), preferred_element_type=F32)
    kt = (k * k_dec).T.astype(BF16)
    s_ref[...] = state * chunk_dec + jnp.dot(kt, v, preferred_element_type=F32)

    o = o + of_ref[...].astype(F32)
    mu = jnp.mean(o, axis=-1, keepdims=True)
    dev = o - mu
    var = jnp.mean(dev * dev, axis=-1, keepdims=True)
    on = dev * lax.rsqrt(var + GN_EPS) * gn_ref[...]
    o_ref[...] = (on * _silu(g_ref[...].astype(F32))).astype(o_ref.dtype)


def _retention(proj, cos, sin, decay_f, decay_b, gn_g, *, batch, tiles, ctx_tiles, d, heads):
    r = proj.shape[0]
    qk_dim = d // heads
    v_dim = 2 * qk_dim
    tc = ROW_TILE
    dec_f = jnp.broadcast_to(decay_f.astype(F32)[:, None, None], (heads, 8, 128))
    dec_b = jnp.broadcast_to(decay_b.astype(F32)[:, None, None], (heads, 8, 128))
    k_col0 = d // qk_dim
    v_col0 = 2 * d // v_dim
    g_col0 = (2 * d + heads * v_dim) // v_dim
    blocks = (2 * tc * qk_dim * 2 + 3 * tc * v_dim * 2 + 2 * tc * qk_dim * 4 + 8 * 128 * 4
              + tc * v_dim * 2 + v_dim * 4)
    temps = qk_dim * v_dim * 4 + 12 * tc * v_dim * 4

    def fwd_row(b, h, c):
        return b * tiles + c

    def bwd_chunk(c):
        return jnp.where(c < ctx_tiles, ctx_tiles - 1 - c, tiles - 1 - (c - ctx_tiles))

    def bwd_row(b, h, c):
        return b * tiles + bwd_chunk(c)

    o_f = pl.pallas_call(
        functools.partial(_ret_fwd_kernel, qk_dim=qk_dim),
        grid=(batch, heads, tiles),
        in_specs=[pl.BlockSpec((tc, qk_dim), lambda b, h, c: (fwd_row(b, h, c), h)),
                  pl.BlockSpec((tc, qk_dim), lambda b, h, c: (fwd_row(b, h, c), k_col0 + h)),
                  pl.BlockSpec((tc, v_dim), lambda b, h, c: (fwd_row(b, h, c), v_col0 + h)),
                  pl.BlockSpec((tc, qk_dim), lambda b, h, c: (c, 0)),
                  pl.BlockSpec((tc, qk_dim), lambda b, h, c: (c, 0)),
                  pl.BlockSpec((1, 8, 128), lambda b, h, c: (h, 0, 0))],
        out_specs=pl.BlockSpec((tc, v_dim), lambda b, h, c: (fwd_row(b, h, c), h)),
        out_shape=jax.ShapeDtypeStruct((r, heads * v_dim), BF16),
        scratch_shapes=[pltpu.VMEM((qk_dim, v_dim), F32)],
        compiler_params=_params(("arbitrary",) * 3, blocks, temps),
        name="retention_fwd",
    )(proj, proj, proj, cos, sin, dec_f)

    return pl.pallas_call(
        functools.partial(_ret_bwd_kernel, qk_dim=qk_dim),
        grid=(batch, heads, tiles),
        in_specs=[pl.BlockSpec((tc, qk_dim), lambda b, h, c: (bwd_row(b, h, c), h)),
                  pl.BlockSpec((tc, qk_dim), lambda b, h, c: (bwd_row(b, h, c), k_col0 + h)),
                  pl.BlockSpec((tc, v_dim), lambda b, h, c: (bwd_row(b, h, c), v_col0 + h)),
                  pl.BlockSpec((tc, qk_dim), lambda b, h, c: (bwd_chunk(c), 0)),
                  pl.BlockSpec((tc, qk_dim), lambda b, h, c: (bwd_chunk(c), 0)),
                  pl.BlockSpec((1, 8, 128), lambda b, h, c: (h, 0, 0)),
                  pl.BlockSpec((tc, v_dim), lambda b, h, c: (bwd_row(b, h, c), h)),
                  pl.BlockSpec((tc, v_dim), lambda b, h, c: (bwd_row(b, h, c), g_col0 + h)),
                  pl.BlockSpec((1, v_dim), lambda b, h, c: (0, h))],
        out_specs=pl.BlockSpec((tc, v_dim), lambda b, h, c: (bwd_row(b, h, c), h)),
        out_shape=jax.ShapeDtypeStruct((r, heads * v_dim), BF16),
        scratch_shapes=[pltpu.VMEM((qk_dim, v_dim), F32)],
        compiler_params=_params(("arbitrary",) * 3, blocks, temps),
        name="retention_bwd_norm_gate",
    )(proj, proj, proj, cos, sin, dec_b, o_f, proj, gn_g.astype(F32).reshape(1, heads * v_dim))


def _final_norm_kernel(x_ref, g_ref, o_ref):
    x = x_ref[...]
    o_ref[...] = x * lax.rsqrt(jnp.mean(x * x, axis=-1, keepdims=True) + NORM_EPS) * g_ref[...]


def _final_norm(xs, g, *, batch, tiles, ctx_tiles):
    d = xs.shape[1]
    seq_tiles = tiles - ctx_tiles
    blocks = 2 * ROW_TILE * d * 4 + d * 4
    return pl.pallas_call(
        _final_norm_kernel,
        grid=(batch, seq_tiles),
        in_specs=[pl.BlockSpec((ROW_TILE, d), lambda b, t: (b * tiles + ctx_tiles + t, 0)),
                  pl.BlockSpec((1, d), lambda b, t: (0, 0))],
        out_specs=pl.BlockSpec((ROW_TILE, d), lambda b, t: (b * seq_tiles + t, 0)),
        out_shape=jax.ShapeDtypeStruct((batch * seq_tiles * ROW_TILE, d), F32),
        compiler_params=_params(("arbitrary", "arbitrary"), blocks, 2 * ROW_TILE * d * 4),
        name="final_norm",
    )(xs, g.reshape(1, d))


def kernel(x, c, ctx, c_ctx, norm_g, mod_w, mod_b, pool_w_in, pool_w_grp, pool_scale, pool_w_out,
           attn_w_in, attn_q_norm, attn_k_norm, attn_w_out, ret_w_in, ret_decay_fwd, ret_decay_bwd,
           ret_gn_g, ret_w_out, final_norm_g):
    batch, seq, d = x.shape
    n_ctx = ctx.shape[1]
    depth = norm_g.shape[0]
    assert seq % ROW_TILE == 0 and n_ctx % ROW_TILE == 0 and seq % GRID_W == 0
    assert batch < MOD_ROWS
    ctx_tiles = n_ctx // ROW_TILE
    tiles = (n_ctx + seq) // ROW_TILE
    rows_per_sample = n_ctx + seq
    tm = _largest_tile(rows_per_sample, (768, 512, 256))

    xs = jnp.concatenate([ctx, x], axis=1).reshape(batch * rows_per_sample, d)
    cvec = jnp.concatenate([c, c_ctx[None, :], jnp.zeros((MOD_ROWS - batch - 1, d), F32)], axis=0)
    mods = _mod_tables(cvec, mod_w, mod_b)
    geom = dict(batch=batch, tiles=tiles, ctx_tiles=ctx_tiles)

    kv_width = (attn_w_in.shape[2] - 2 * d) // 2
    ret_heads = ret_decay_fwd.shape[1]
    attn_cos, attn_sin = _rope_tables(n_ctx, seq, ATTN_HEAD_DIM)
    ret_cos, ret_sin = _rope_tables(n_ctx, seq, d // ret_heads)

    for i in range(depth):
        kind, j = i % 3, i // 3
        h = _norm_mod(xs, norm_g[i], mods[i], **geom)
        if kind == 0:
            ug = _matmul(h, pool_w_in[j].astype(BF16), tm=tm)
            z = _pool_mix(ug, pool_w_grp[j].astype(BF16), pool_scale[j], **geom)
            w_out = pool_w_out[j]
        elif kind == 1:
            proj = _matmul(h, attn_w_in[j].astype(BF16), tm=tm)
            qk = _qk_prep(proj, attn_cos, attn_sin, attn_q_norm[j], attn_k_norm[j],
                          batch=batch, tiles=tiles, d=d, kv_width=kv_width)
            z = _attention(qk, proj, **geom, d=d, kv_width=kv_width)
            w_out = attn_w_out[j]
        else:
            proj = _matmul(h, ret_w_in[j].astype(BF16), tm=tm)
            z = _retention(proj, ret_cos, ret_sin, ret_decay_fwd[j], ret_decay_bwd[j], ret_gn_g[j],
                           **geom, d=d, heads=ret_heads)
            w_out = ret_w_out[j]
        xs = _matmul_residual(z, w_out.astype(BF16), xs, mods[i], batch=batch, tm=tm, ctx_rows=n_ctx)

    out = _final_norm(xs, final_norm_g, **geom)
    return out.reshape(batch, seq, d)
```

```python
import functools
import math

import jax
import jax.numpy as jnp
from jax import lax
from jax.experimental import pallas as pl
from jax.experimental.pallas import tpu as pltpu

F32 = jnp.float32
BF16 = jnp.bfloat16

GRID_W = 64
ATTN_HEAD_DIM = 128
POOL_WINDOWS = (2, 4, 8, 16)
ROPE_THETA = 10000.0
NORM_EPS = 1e-6
GN_EPS = 1e-5

RET_HEADS_PER_STEP = 2
ATTN_SCORES_AHEAD = 2
ROW_TILE = 256
HALO_ROWS = 16
EDGE = max(POOL_WINDOWS) // 2
MOD_ROWS = 8
V7X_VMEM_BYTES = 64 * 1024 * 1024
VMEM_CAP_BYTES = V7X_VMEM_BYTES - 8 * 1024 * 1024


def _vmem_limit(block_bytes, temp_bytes=0):
    need = 2 * block_bytes + temp_bytes + 4 * 1024 * 1024
    return int(min(max(need, 16 * 1024 * 1024), VMEM_CAP_BYTES))


def _params(semantics, block_bytes, temp_bytes=0):
    return pltpu.CompilerParams(dimension_semantics=semantics,
                                vmem_limit_bytes=_vmem_limit(block_bytes, temp_bytes))


def _largest_tile(n, candidates):
    for c in candidates:
        if n % c == 0:
            return c
    raise ValueError(f"no tile among {candidates} divides {n}")


def _silu(v):
    return v * (1.0 / (1.0 + jnp.exp(-v)))


def _mod_kernel(c_ref, w_ref, b_ref, o_ref):
    a = _silu(c_ref[...]).astype(BF16)
    w = w_ref[0].astype(BF16)
    o_ref[0] = jnp.dot(a, w, preferred_element_type=F32) + b_ref[0]


def _mod_tables(cvec, mod_w, mod_b):
    depth, d, n = mod_w.shape
    tn = _largest_tile(n, (512, 256, 128))
    blocks = MOD_ROWS * d * 4 + d * tn * 4 + tn * 4 + MOD_ROWS * tn * 4
    return pl.pallas_call(
        _mod_kernel,
        grid=(depth, n // tn),
        in_specs=[pl.BlockSpec((MOD_ROWS, d), lambda l, j: (0, 0)),
                  pl.BlockSpec((1, d, tn), lambda l, j: (l, 0, j)),
                  pl.BlockSpec((1, 1, tn), lambda l, j: (l, 0, j))],
        out_specs=pl.BlockSpec((1, MOD_ROWS, tn), lambda l, j: (l, 0, j)),
        out_shape=jax.ShapeDtypeStruct((depth, MOD_ROWS, n), F32),
        compiler_params=_params(("arbitrary", "arbitrary"), blocks, d * tn * 2),
        name="adaln_tables",
    )(cvec, mod_w, mod_b.reshape(depth, 1, n))


def _norm_mod_kernel(x_ref, g_ref, mod_ref, o_ref, *, d, ctx_tiles, ctx_row):
    b, t = pl.program_id(0), pl.program_id(1)
    row = jnp.where(t < ctx_tiles, ctx_row, b)
    shift = mod_ref[pl.ds(row, 1), 0:d]
    scale = mod_ref[pl.ds(row, 1), d:2 * d]
    x = x_ref[...]
    y = x * lax.rsqrt(jnp.mean(x * x, axis=-1, keepdims=True) + NORM_EPS) * g_ref[...]
    o_ref[...] = (y * (1.0 + scale) + shift).astype(o_ref.dtype)


def _norm_mod(xs, g, mods, *, batch, tiles, ctx_tiles):
    r, d = xs.shape
    blocks = ROW_TILE * d * 4 + d * 4 + MOD_ROWS * 3 * d * 4 + ROW_TILE * d * 2
    return pl.pallas_call(
        functools.partial(_norm_mod_kernel, d=d, ctx_tiles=ctx_tiles, ctx_row=batch),
        grid=(batch, tiles),
        in_specs=[pl.BlockSpec((ROW_TILE, d), lambda b, t: (b * tiles + t, 0)),
                  pl.BlockSpec((1, d), lambda b, t: (0, 0)),
                  pl.BlockSpec((MOD_ROWS, 3 * d), lambda b, t: (0, 0))],
        out_specs=pl.BlockSpec((ROW_TILE, d), lambda b, t: (b * tiles + t, 0)),
        out_shape=jax.ShapeDtypeStruct((r, d), BF16),
        compiler_params=_params(("arbitrary", "arbitrary"), blocks, 3 * ROW_TILE * d * 4),
        name="norm_modulate",
    )(xs, g.reshape(1, d), mods)


def _matmul_kernel(a_ref, b_ref, o_ref):
    o_ref[...] = jnp.dot(a_ref[...], b_ref[...], preferred_element_type=F32).astype(o_ref.dtype)


def _matmul(a, w, *, tm, out_dtype=BF16):
    m, k = a.shape
    n = w.shape[1]
    tn = _largest_tile(n, (1024, 512, 256, 128))
    blocks = tm * k * 2 + k * tn * 2 + tm * tn * jnp.dtype(out_dtype).itemsize
    return pl.pallas_call(
        _matmul_kernel,
        grid=(m // tm, n // tn),
        in_specs=[pl.BlockSpec((tm, k), lambda i, j: (i, 0)),
                  pl.BlockSpec((k, tn), lambda i, j: (0, j))],
        out_specs=pl.BlockSpec((tm, tn), lambda i, j: (i, j)),
        out_shape=jax.ShapeDtypeStruct((m, n), out_dtype),
        compiler_params=_params(("arbitrary", "arbitrary"), blocks, tm * tn * 4),
        name="in_proj",
    )(a, w)


def _matmul_res_kernel(a_ref, b_ref, x_ref, gate_ref, o_ref, *, tm, ctx_rows, ctx_row):
    b, t = pl.program_id(0), pl.program_id(1)
    y = jnp.dot(a_ref[...], b_ref[...], preferred_element_type=F32)
    gate_b = gate_ref[pl.ds(b, 1), :]
    gate_c = gate_ref[pl.ds(ctx_row, 1), :]
    rows = t * tm + lax.broadcasted_iota(jnp.int32, (tm, 1), 0)
    gate = jnp.where(rows < ctx_rows, gate_c, gate_b)
    o_ref[...] = x_ref[...] + gate * y


def _matmul_residual(z, w, xs, mods, *, batch, tm, ctx_rows):
    m, kdim = z.shape
    d = w.shape[1]
    for tn in (1024, 512, 256, 128):
        blocks = tm * kdim * 2 + kdim * tn * 2 + 2 * tm * tn * 4 + MOD_ROWS * tn * 4
        if d % tn == 0 and 2 * blocks + 2 * tm * tn * 4 + 4 * 1024 * 1024 <= VMEM_CAP_BYTES:
            break
    else:
        raise ValueError("out-projection blocks do not fit VMEM")
    tiles = m // (batch * tm)
    gate_col0 = 2 * d // tn
    return pl.pallas_call(
        functools.partial(_matmul_res_kernel, tm=tm, ctx_rows=ctx_rows, ctx_row=batch),
        grid=(batch, tiles, d // tn),
        in_specs=[pl.BlockSpec((tm, kdim), lambda b, t, j: (b * tiles + t, 0)),
                  pl.BlockSpec((kdim, tn), lambda b, t, j: (0, j)),
                  pl.BlockSpec((tm, tn), lambda b, t, j: (b * tiles + t, j)),
                  pl.BlockSpec((MOD_ROWS, tn), lambda b, t, j: (0, gate_col0 + j))],
        out_specs=pl.BlockSpec((tm, tn), lambda b, t, j: (b * tiles + t, j)),
        out_shape=jax.ShapeDtypeStruct((m, d), F32),
        input_output_aliases={2: 0},
        compiler_params=_params(("arbitrary",) * 3, blocks, 2 * tm * tn * 4),
        name="out_proj_residual",
    )(z, w, xs, mods)


def _edge_window_sums(x, window):
    half = window // 2
    total = x[EDGE - half:2 * EDGE - half, :]
    for off in range(-half + 1, window - half):
        total = total + x[EDGE + off:2 * EDGE + off, :]
    return total


def _pool_kernel(up_ref, u_ref, un_ref, g_ref, s_ref, o_ref, *,
                 ctx_tiles, tiles, ctx_rows, seq_rows):
    gi, t = pl.program_id(0), pl.program_id(2)
    tr = ROW_TILE
    in_ctx = t < ctx_tiles
    first = (t == 0) | (t == ctx_tiles)
    last = (t == ctx_tiles - 1) | (t == tiles - 1)
    seq_tile = jnp.where(in_ctx, t, t - ctx_tiles)
    n_seq = jnp.where(in_ctx, ctx_rows, seq_rows)

    ub = u_ref[...]
    u = ub.astype(F32)
    above = jnp.where(first, 0.0, up_ref[HALO_ROWS - EDGE:HALO_ROWS, :].astype(F32))
    below = jnp.where(last, 0.0, un_ref[0:EDGE, :].astype(F32))
    top = jnp.concatenate([above, u[0:2 * EDGE, :]], axis=0)
    bottom = jnp.concatenate([u[tr - 2 * EDGE:tr, :], below], axis=0)
    pos = seq_tile * tr + lax.broadcasted_iota(jnp.int32, (tr, 1), 0)
    delta = lax.broadcasted_iota(jnp.int32, (tr, tr), 1) - lax.broadcasted_iota(jnp.int32, (tr, tr), 0)

    for gidx, window in enumerate(POOL_WINDOWS):
        @pl.when(gi == gidx)
        def _(window=window):
            half = window // 2
            band = jnp.where((delta >= -half) & (delta < window - half), 1.0, 0.0).astype(BF16)
            s = jnp.dot(band, ub, preferred_element_type=F32)
            s = jnp.concatenate([_edge_window_sums(top, window), s[EDGE:tr - EDGE, :],
                                 _edge_window_sums(bottom, window)], axis=0)
            cnt = jnp.minimum(pos + (window - half), n_seq) - jnp.maximum(pos - half, 0)
            mixed = s * (1.0 / cnt.astype(F32)) - u
            gate = _silu(g_ref[...].astype(F32))
            o_ref[...] = (mixed * s_ref[...] * gate).astype(o_ref.dtype)


def _fold_group_mix(w_in, w_grp):
    d = w_in.shape[0]
    n_groups, gdim, _ = w_grp.shape
    tm = _largest_tile(d, (512, 256, 128))
    blocks = 2 * tm * gdim * 2 + gdim * gdim * 2
    return pl.pallas_call(
        _matmul_kernel,
        grid=(n_groups, d // tm),
        in_specs=[pl.BlockSpec((tm, gdim), lambda g, i: (i, g)),
                  pl.BlockSpec((None, gdim, gdim), lambda g, i: (g, 0, 0))],
        out_specs=pl.BlockSpec((tm, gdim), lambda g, i: (i, g)),
        out_shape=jax.ShapeDtypeStruct(w_in.shape, BF16),
        input_output_aliases={0: 0},
        compiler_params=_params(("arbitrary", "arbitrary"), blocks, tm * gdim * 4),
        name="fold_group_mix",
    )(w_in, w_grp)


def _pool_mix(ug, scale, *, n_groups, batch, tiles, ctx_tiles):
    r = ug.shape[0]
    width = ug.shape[1] // 2
    gdim = width // n_groups
    tr = ROW_TILE
    hb = tr // HALO_ROWS
    n_halo_blocks = r // HALO_ROWS
    kern = functools.partial(_pool_kernel, ctx_tiles=ctx_tiles, tiles=tiles,
                             ctx_rows=ctx_tiles * tr, seq_rows=(tiles - ctx_tiles) * tr)
    blocks = 3 * tr * gdim * 2 + 2 * HALO_ROWS * gdim * 2 + gdim * 4
    return pl.pallas_call(
        kern,
        grid=(n_groups, batch, tiles),
        in_specs=[pl.BlockSpec((HALO_ROWS, gdim),
                               lambda g, b, t: (jnp.maximum((b * tiles + t) * hb - 1, 0), g)),
                  pl.BlockSpec((tr, gdim), lambda g, b, t: (b * tiles + t, g)),
                  pl.BlockSpec((HALO_ROWS, gdim),
                               lambda g, b, t: (jnp.minimum((b * tiles + t + 1) * hb, n_halo_blocks - 1), g)),
                  pl.BlockSpec((tr, gdim), lambda g, b, t: (b * tiles + t, n_groups + g)),
                  pl.BlockSpec((1, gdim), lambda g, b, t: (0, g))],
        out_specs=pl.BlockSpec((tr, gdim), lambda g, b, t: (b * tiles + t, g)),
        out_shape=jax.ShapeDtypeStruct((r, width), BF16),
        compiler_params=_params(("arbitrary",) * 3, blocks, 8 * tr * gdim * 4),
        name="pool_mix",
    )(ug, ug, ug, ug, scale.reshape(1, width))


def _rope_tables(n_ctx, n_seq, head_dim):
    rows = n_seq // GRID_W
    row = jnp.repeat(jnp.arange(rows, dtype=F32), GRID_W)
    col = jnp.tile(jnp.arange(GRID_W, dtype=F32), rows)
    n_freq = head_dim // 4
    inv = ROPE_THETA ** (-jnp.arange(n_freq, dtype=F32) / n_freq)
    ang = jnp.concatenate([row[:, None] * inv, col[:, None] * inv], axis=-1)
    ang = jnp.concatenate([jnp.zeros((n_ctx, head_dim // 2), F32), ang], axis=0)
    cos, sin = jnp.cos(ang), jnp.sin(ang)
    return jnp.concatenate([cos, cos], axis=-1), jnp.concatenate([-sin, sin], axis=-1)


def _split_pairs(w, n_cols, head_dim):
    lead = w.shape[:-1]
    part = w[..., :n_cols].reshape(*lead, n_cols // head_dim, head_dim // 2, 2)
    part = jnp.swapaxes(part, -1, -2).reshape(*lead, n_cols)
    return jnp.concatenate([part, w[..., n_cols:]], axis=-1)


def _rope(x, cos2, sin_signed):
    half = x.shape[-1] // 2
    return x * cos2 + pltpu.roll(x, half, axis=1) * sin_signed


def _qk_prep_kernel(x_ref, cos_ref, sin_ref, qg_ref, kg_ref, o_ref, *, q_blocks, heads_per_block, scale):
    j = pl.program_id(2)
    is_q = j < q_blocks
    g = jnp.where(is_q, qg_ref[...], kg_ref[...])
    out_scale = jnp.where(is_q, scale, 1.0)
    cos, sin = cos_ref[...], sin_ref[...]
    x = x_ref[...].astype(F32)
    hd = ATTN_HEAD_DIM
    outs = []
    for h in range(heads_per_block):
        xh = x[:, h * hd:(h + 1) * hd]
        y = xh * lax.rsqrt(jnp.mean(xh * xh, axis=-1, keepdims=True) + NORM_EPS) * g
        outs.append(_rope(y, cos, sin) * out_scale)
    o_ref[...] = jnp.concatenate(outs, axis=1).astype(o_ref.dtype)


def _qk_prep(proj, cos, sin, q_g, k_g, *, batch, tiles, d, kv_width):
    r = proj.shape[0]
    hd = ATTN_HEAD_DIM
    heads_per_block = 4
    bw = heads_per_block * hd
    width = d + kv_width
    kern = functools.partial(_qk_prep_kernel, q_blocks=d // bw, heads_per_block=heads_per_block,
                             scale=hd ** -0.5 * math.log2(math.e))
    blocks = 2 * ROW_TILE * bw * 2 + 2 * ROW_TILE * hd * 4 + 2 * hd * 4
    return pl.pallas_call(
        kern,
        grid=(batch, tiles, width // bw),
        in_specs=[pl.BlockSpec((ROW_TILE, bw), lambda b, t, j: (b * tiles + t, j)),
                  pl.BlockSpec((ROW_TILE, hd), lambda b, t, j: (t, 0)),
                  pl.BlockSpec((ROW_TILE, hd), lambda b, t, j: (t, 0)),
                  pl.BlockSpec((1, hd), lambda b, t, j: (0, 0)),
                  pl.BlockSpec((1, hd), lambda b, t, j: (0, 0))],
        out_specs=pl.BlockSpec((ROW_TILE, bw), lambda b, t, j: (b * tiles + t, j)),
        out_shape=jax.ShapeDtypeStruct((r, width), BF16),
        compiler_params=_params(("arbitrary",) * 3, blocks, 8 * ROW_TILE * bw * 4),
        name="qk_norm_rope",
    )(proj, cos, sin, q_g.reshape(1, hd), k_g.reshape(1, hd))


def _v_transpose_kernel(v_ref, o_ref):
    o_ref[...] = v_ref[...].astype(F32).T.astype(o_ref.dtype)


def _v_transpose(proj, *, batch, n_chunks, kv_chunk, d, kv_width):
    hd = ATTN_HEAD_DIM
    kv_heads = kv_width // hd
    v_col0 = (d + kv_width) // hd
    return pl.pallas_call(
        _v_transpose_kernel,
        grid=(batch, kv_heads, n_chunks),
        in_specs=[pl.BlockSpec((kv_chunk, hd), lambda b, h, c: (b * n_chunks + c, v_col0 + h))],
        out_specs=pl.BlockSpec((None, None, None, hd, kv_chunk), lambda b, h, c: (b, h, c, 0, 0)),
        out_shape=jax.ShapeDtypeStruct((batch, kv_heads, n_chunks, hd, kv_chunk), BF16),
        compiler_params=_params(("arbitrary",) * 3, 2 * kv_chunk * hd * 2, 4 * kv_chunk * hd * 4),
        name="v_transpose",
    )(proj)


def _attn_kernel(q_ref, k_ref, vt_ref, g_ref, o_ref, s_ref, smax_ref, m_ref, l_ref, acc_ref, *,
                 groups, ctx_tiles, ctx_rows, n_chunks, kv_chunk):
    t = pl.program_id(2)
    hd = ATTN_HEAD_DIM
    nt = (((1,), (1,)), ((), ()))

    m_ref[...] = jnp.full(m_ref.shape, -jnp.inf, F32)
    l_ref[...] = jnp.zeros(l_ref.shape, F32)
    acc_ref[...] = jnp.zeros(acc_ref.shape, F32)

    def scores(k, slot, n, h):
        s = lax.dot_general(k, q_ref[:, h * hd:(h + 1) * hd], nt, preferred_element_type=F32)
        s_ref[slot, h, 0:n, :] = s
        smax_ref[slot, h] = jnp.max(s, axis=0, keepdims=True)

    def softmax_pv(vt, slot, n, h):
        s = s_ref[slot, h, 0:n, :]
        m_prev = m_ref[h]
        m_new = jnp.maximum(m_prev, smax_ref[slot, h])
        alpha = jnp.exp2(m_prev - m_new)
        p = jnp.exp2(s - m_new)
        l_ref[h] = alpha * l_ref[h] + jnp.sum(p, axis=0, keepdims=True)
        m_ref[h] = m_new
        acc_ref[h] = alpha * acc_ref[h] + jnp.dot(vt, p.astype(BF16), preferred_element_type=F32)

    def chunk_step(k_next, vt, slot, n):
        ahead = min(ATTN_SCORES_AHEAD, groups)
        if k_next is not None:
            for h in range(ahead):
                scores(k_next, 1 - slot, n, h)
        for h in range(groups):
            softmax_pv(vt, slot, n, h)
            if k_next is not None and h + ahead < groups:
                scores(k_next, 1 - slot, n, h + ahead)

    def k_chunk(c):
        return k_ref[pl.ds(pl.multiple_of(c * kv_chunk, kv_chunk), kv_chunk), :]

    @pl.when(t < ctx_tiles)
    def _():
        for r0 in range(0, ctx_rows, ROW_TILE):
            c, off = divmod(r0, kv_chunk)
            for h in range(groups):
                scores(k_ref[r0:r0 + ROW_TILE, :], 0, ROW_TILE, h)
            chunk_step(None, vt_ref[c, :, off:off + ROW_TILE], 0, ROW_TILE)

    @pl.when(t >= ctx_tiles)
    def _():
        for h in range(groups):
            scores(k_chunk(0), 0, kv_chunk, h)
        pairs = (n_chunks - 1) // 2

        def body(i, carry):
            c = 2 * i
            chunk_step(k_chunk(c + 1), vt_ref[c], 0, kv_chunk)
            chunk_step(k_chunk(c + 2), vt_ref[c + 1], 1, kv_chunk)
            return carry
        lax.fori_loop(0, pairs, body, 0)
        if n_chunks % 2 == 0:
            chunk_step(k_chunk(n_chunks - 1), vt_ref[n_chunks - 2], 0, kv_chunk)
            chunk_step(None, vt_ref[n_chunks - 1], 1, kv_chunk)
        else:
            chunk_step(None, vt_ref[n_chunks - 1], 0, kv_chunk)

    o = jnp.concatenate([(acc_ref[h] * (1.0 / l_ref[h])).T for h in range(groups)], axis=1)
    o_ref[...] = (o * _silu(g_ref[...].astype(F32))).astype(o_ref.dtype)


def _attention(qk, proj, *, batch, tiles, ctx_tiles, d, kv_width):
    r = qk.shape[0]
    hd = ATTN_HEAD_DIM
    kv_heads = kv_width // hd
    groups = d // kv_width
    bw = groups * hd
    kv_rows = tiles * ROW_TILE
    kv_chunk = _largest_tile(kv_rows, (768, 512, 256))
    n_chunks = kv_rows // kv_chunk
    vt = _v_transpose(proj, batch=batch, n_chunks=n_chunks, kv_chunk=kv_chunk, d=d, kv_width=kv_width)
    kern = functools.partial(_attn_kernel, groups=groups, ctx_tiles=ctx_tiles,
                             ctx_rows=ctx_tiles * ROW_TILE, n_chunks=n_chunks, kv_chunk=kv_chunk)
    blocks = 3 * ROW_TILE * bw * 2 + 2 * kv_rows * hd * 2
    temps = (2 * groups + 6) * kv_chunk * ROW_TILE * 4 + groups * (hd + 16) * ROW_TILE * 4
    return pl.pallas_call(
        kern,
        grid=(batch, kv_heads, tiles),
        in_specs=[pl.BlockSpec((ROW_TILE, bw), lambda b, h, t: (b * tiles + t, h)),
                  pl.BlockSpec((kv_rows, hd), lambda b, h, t: (b, d // hd + h)),
                  pl.BlockSpec((None, None, n_chunks, hd, kv_chunk), lambda b, h, t: (b, h, 0, 0, 0)),
                  pl.BlockSpec((ROW_TILE, bw), lambda b, h, t: (b * tiles + t, (d + 2 * kv_width) // bw + h))],
        out_specs=pl.BlockSpec((ROW_TILE, bw), lambda b, h, t: (b * tiles + t, h)),
        out_shape=jax.ShapeDtypeStruct((r, d), BF16),
        scratch_shapes=[pltpu.VMEM((2, groups, kv_chunk, ROW_TILE), F32),
                        pltpu.VMEM((2, groups, 1, ROW_TILE), F32),
                        pltpu.VMEM((groups, 1, ROW_TILE), F32),
                        pltpu.VMEM((groups, 1, ROW_TILE), F32),
                        pltpu.VMEM((groups, hd, ROW_TILE), F32)],
        compiler_params=_params(("arbitrary",) * 3, blocks, temps),
        name="attention",
    )(qk, qk, vt, proj)


def _ret_chunk(q_ref, k_ref, v_ref, cos_ref, sin_ref, dec_ref, s_ref, *, qk_dim, hps, reverse):
    tc = ROW_TILE
    v_dim = 2 * qk_dim
    nt = (((1,), (1,)), ((), ()))
    cos, sin = cos_ref[...], sin_ref[...]
    ii = lax.broadcasted_iota(jnp.int32, (tc, tc), 0)
    jj = lax.broadcasted_iota(jnp.int32, (tc, tc), 1)
    pos = lax.broadcasted_iota(jnp.int32, (tc, 1), 0).astype(F32)
    if reverse:
        dist, keep = (jj - ii), jj > ii
        q_pow, k_pow = tc - pos, pos
    else:
        dist, keep = (ii - jj), ii >= jj
        q_pow, k_pow = pos + 1.0, tc - 1.0 - pos
    dist = jnp.maximum(dist, 0).astype(F32)

    heads = []
    for h in range(hps):
        log_g = jnp.log1p(-jnp.exp(dec_ref[h]))[0:1, 0:1]
        q = _rope(q_ref[:, h * qk_dim:(h + 1) * qk_dim].astype(F32), cos, sin)
        k = _rope(k_ref[:, h * qk_dim:(h + 1) * qk_dim].astype(F32), cos, sin) * (qk_dim ** -0.5)
        scores = lax.dot_general(q.astype(BF16), k.astype(BF16), nt, preferred_element_type=F32)
        heads.append((log_g, q, k, scores))

    outs = []
    for h, (log_g, q, k, scores) in enumerate(heads):
        v = v_ref[:, h * v_dim:(h + 1) * v_dim]
        decay = jnp.where(keep, jnp.exp(dist * log_g), 0.0)
        lhs = jnp.concatenate([(scores * decay).astype(BF16),
                               (q * jnp.exp(q_pow * log_g)).astype(BF16)], axis=1)
        rhs = jnp.concatenate([v, s_ref[h].astype(BF16)], axis=0)
        outs.append(jnp.dot(lhs, rhs, preferred_element_type=F32))

    for h, (log_g, q, k, scores) in enumerate(heads):
        v = v_ref[:, h * v_dim:(h + 1) * v_dim]
        kt = (k * jnp.exp(k_pow * log_g)).T.astype(BF16)
        s_ref[h] = s_ref[h] * jnp.exp(tc * log_g) + jnp.dot(kt, v, preferred_element_type=F32)
    return outs


def _ret_fwd_kernel(q_ref, k_ref, v_ref, cos_ref, sin_ref, dec_ref, o_ref, s_ref, *, qk_dim, hps):
    @pl.when(pl.program_id(2) == 0)
    def _():
        s_ref[...] = jnp.zeros(s_ref.shape, F32)

    outs = _ret_chunk(q_ref, k_ref, v_ref, cos_ref, sin_ref, dec_ref, s_ref,
                      qk_dim=qk_dim, hps=hps, reverse=False)
    o_ref[...] = jnp.concatenate(outs, axis=1).astype(o_ref.dtype)


def _ret_bwd_kernel(q_ref, k_ref, v_ref, cos_ref, sin_ref, dec_ref, of_ref, g_ref, gn_ref, o_ref, s_ref,
                    *, qk_dim, hps):
    @pl.when(pl.program_id(2) == 0)
    def _():
        s_ref[...] = jnp.zeros(s_ref.shape, F32)

    outs = _ret_chunk(q_ref, k_ref, v_ref, cos_ref, sin_ref, dec_ref, s_ref,
                      qk_dim=qk_dim, hps=hps, reverse=True)
    v_dim = 2 * qk_dim
    normed = []
    for h, o in enumerate(outs):
        cols = slice(h * v_dim, (h + 1) * v_dim)
        o = o + of_ref[:, cols].astype(F32)
        mu = jnp.mean(o, axis=-1, keepdims=True)
        dev = o - mu
        var = jnp.mean(dev * dev, axis=-1, keepdims=True)
        normed.append(dev * lax.rsqrt(var + GN_EPS) * gn_ref[:, cols])
    o_ref[...] = (jnp.concatenate(normed, axis=1) * _silu(g_ref[...].astype(F32))).astype(o_ref.dtype)


def _retention(proj, cos, sin, decay_f, decay_b, gn_g, *, batch, tiles, ctx_tiles, d, heads):
    r = proj.shape[0]
    qk_dim = d // heads
    v_dim = 2 * qk_dim
    tc = ROW_TILE
    dec_f = jnp.broadcast_to(decay_f.astype(F32)[:, None, None], (heads, 8, 128))
    dec_b = jnp.broadcast_to(decay_b.astype(F32)[:, None, None], (heads, 8, 128))
    hps = _largest_tile(heads, (RET_HEADS_PER_STEP, 1))
    qw, vw = hps * qk_dim, hps * v_dim
    k_col0 = d // qw
    v_col0 = 2 * d // vw
    g_col0 = (2 * d + heads * v_dim) // vw
    blocks = 2 * tc * qw * 2 + 4 * tc * vw * 2 + 2 * tc * qk_dim * 4 + hps * 8 * 128 * 4 + vw * 4
    temps = hps * (qk_dim * v_dim * 4 + 12 * tc * v_dim * 4)

    def fwd_row(b, h, c):
        return b * tiles + c

    def bwd_chunk(c):
        return jnp.where(c < ctx_tiles, ctx_tiles - 1 - c, tiles - 1 - (c - ctx_tiles))

    def bwd_row(b, h, c):
        return b * tiles + bwd_chunk(c)

    o_f = pl.pallas_call(
        functools.partial(_ret_fwd_kernel, qk_dim=qk_dim, hps=hps),
        grid=(batch, heads // hps, tiles),
        in_specs=[pl.BlockSpec((tc, qw), lambda b, h, c: (fwd_row(b, h, c), h)),
                  pl.BlockSpec((tc, qw), lambda b, h, c: (fwd_row(b, h, c), k_col0 + h)),
                  pl.BlockSpec((tc, vw), lambda b, h, c: (fwd_row(b, h, c), v_col0 + h)),
                  pl.BlockSpec((tc, qk_dim), lambda b, h, c: (c, 0)),
                  pl.BlockSpec((tc, qk_dim), lambda b, h, c: (c, 0)),
                  pl.BlockSpec((hps, 8, 128), lambda b, h, c: (h, 0, 0))],
        out_specs=pl.BlockSpec((tc, vw), lambda b, h, c: (fwd_row(b, h, c), h)),
        out_shape=jax.ShapeDtypeStruct((r, heads * v_dim), BF16),
        scratch_shapes=[pltpu.VMEM((hps, qk_dim, v_dim), F32)],
        compiler_params=_params(("arbitrary",) * 3, blocks, temps),
        name="retention_fwd",
    )(proj, proj, proj, cos, sin, dec_f)

    return pl.pallas_call(
        functools.partial(_ret_bwd_kernel, qk_dim=qk_dim, hps=hps),
        grid=(batch, heads // hps, tiles),
        in_specs=[pl.BlockSpec((tc, qw), lambda b, h, c: (bwd_row(b, h, c), h)),
                  pl.BlockSpec((tc, qw), lambda b, h, c: (bwd_row(b, h, c), k_col0 + h)),
                  pl.BlockSpec((tc, vw), lambda b, h, c: (bwd_row(b, h, c), v_col0 + h)),
                  pl.BlockSpec((tc, qk_dim), lambda b, h, c: (bwd_chunk(c), 0)),
                  pl.BlockSpec((tc, qk_dim), lambda b, h, c: (bwd_chunk(c), 0)),
                  pl.BlockSpec((hps, 8, 128), lambda b, h, c: (h, 0, 0)),
                  pl.BlockSpec((tc, vw), lambda b, h, c: (bwd_row(b, h, c), h)),
                  pl.BlockSpec((tc, vw), lambda b, h, c: (bwd_row(b, h, c), g_col0 + h)),
                  pl.BlockSpec((1, vw), lambda b, h, c: (0, h))],
        out_specs=pl.BlockSpec((tc, vw), lambda b, h, c: (bwd_row(b, h, c), h)),
        out_shape=jax.ShapeDtypeStruct((r, heads * v_dim), BF16),
        scratch_shapes=[pltpu.VMEM((hps, qk_dim, v_dim), F32)],
        compiler_params=_params(("arbitrary",) * 3, blocks, temps),
        name="retention_bwd_norm_gate",
    )(proj, proj, proj, cos, sin, dec_b, o_f, proj, gn_g.astype(F32).reshape(1, heads * v_dim))


def _final_norm_kernel(x_ref, g_ref, o_ref):
    x = x_ref[...]
    o_ref[...] = x * lax.rsqrt(jnp.mean(x * x, axis=-1, keepdims=True) + NORM_EPS) * g_ref[...]


def _final_norm(xs, g, *, batch, tiles, ctx_tiles):
    d = xs.shape[1]
    seq_tiles = tiles - ctx_tiles
    blocks = 2 * ROW_TILE * d * 4 + d * 4
    return pl.pallas_call(
        _final_norm_kernel,
        grid=(batch, seq_tiles),
        in_specs=[pl.BlockSpec((ROW_TILE, d), lambda b, t: (b * tiles + ctx_tiles + t, 0)),
                  pl.BlockSpec((1, d), lambda b, t: (0, 0))],
        out_specs=pl.BlockSpec((ROW_TILE, d), lambda b, t: (b * seq_tiles + t, 0)),
        out_shape=jax.ShapeDtypeStruct((batch * seq_tiles * ROW_TILE, d), F32),
        compiler_params=_params(("arbitrary", "arbitrary"), blocks, 2 * ROW_TILE * d * 4),
        name="final_norm",
    )(xs, g.reshape(1, d))


def kernel(x, c, ctx, c_ctx, norm_g, mod_w, mod_b, pool_w_in, pool_w_grp, pool_scale, pool_w_out,
           attn_w_in, attn_q_norm, attn_k_norm, attn_w_out, ret_w_in, ret_decay_fwd, ret_decay_bwd,
           ret_gn_g, ret_w_out, final_norm_g):
    batch, seq, d = x.shape
    n_ctx = ctx.shape[1]
    depth = norm_g.shape[0]
    assert seq % ROW_TILE == 0 and n_ctx % ROW_TILE == 0 and seq % GRID_W == 0
    assert batch < MOD_ROWS
    ctx_tiles = n_ctx // ROW_TILE
    tiles = (n_ctx + seq) // ROW_TILE
    rows_per_sample = n_ctx + seq
    tm = _largest_tile(rows_per_sample, (768, 512, 256))

    xs = jnp.concatenate([ctx, x], axis=1).reshape(batch * rows_per_sample, d)
    cvec = jnp.concatenate([c, c_ctx[None, :], jnp.zeros((MOD_ROWS - batch - 1, d), F32)], axis=0)
    mods = _mod_tables(cvec, mod_w, mod_b)
    geom = dict(batch=batch, tiles=tiles, ctx_tiles=ctx_tiles)

    kv_width = (attn_w_in.shape[2] - 2 * d) // 2
    ret_heads = ret_decay_fwd.shape[1]
    attn_cos, attn_sin = _rope_tables(n_ctx, seq, ATTN_HEAD_DIM)
    ret_cos, ret_sin = _rope_tables(n_ctx, seq, d // ret_heads)

    for i in range(depth):
        kind, j = i % 3, i // 3
        h = _norm_mod(xs, norm_g[i], mods[i], **geom)
        if kind == 0:
            w_in = _fold_group_mix(pool_w_in[j].astype(BF16), pool_w_grp[j].astype(BF16))
            ug = _matmul(h, w_in, tm=tm)
            z = _pool_mix(ug, pool_scale[j], n_groups=pool_w_grp.shape[1], **geom)
            w_out = pool_w_out[j]
        elif kind == 1:
            w_in = _split_pairs(attn_w_in[j], d + kv_width, ATTN_HEAD_DIM).astype(BF16)
            proj = _matmul(h, w_in, tm=tm)
            qk = _qk_prep(proj, attn_cos, attn_sin,
                          _split_pairs(attn_q_norm[j], ATTN_HEAD_DIM, ATTN_HEAD_DIM),
                          _split_pairs(attn_k_norm[j], ATTN_HEAD_DIM, ATTN_HEAD_DIM),
                          batch=batch, tiles=tiles, d=d, kv_width=kv_width)
            z = _attention(qk, proj, **geom, d=d, kv_width=kv_width)
            w_out = attn_w_out[j]
        else:
            w_in = _split_pairs(ret_w_in[j], 2 * d, d // ret_heads).astype(BF16)
            proj = _matmul(h, w_in, tm=tm)
            z = _retention(proj, ret_cos, ret_sin, ret_decay_fwd[j], ret_decay_bwd[j], ret_gn_g[j],
                           **geom, d=d, heads=ret_heads)
            w_out = ret_w_out[j]
        xs = _matmul_residual(z, w_out.astype(BF16), xs, mods[i], batch=batch, tm=tm, ctx_rows=n_ctx)

    out = _final_norm(xs, final_norm_g, **geom)
    return out.reshape(batch, seq, d)
```

```python
import functools
import math

import jax
import jax.numpy as jnp
from jax import lax
from jax.experimental import pallas as pl
from jax.experimental.pallas import tpu as pltpu

F32 = jnp.float32
BF16 = jnp.bfloat16

GRID_W = 64
ATTN_HEAD_DIM = 128
POOL_WINDOWS = (2, 4, 8, 16)
ROPE_THETA = 10000.0
NORM_EPS = 1e-6
GN_EPS = 1e-5

RET_HEADS_PER_STEP = 4
ATTN_SCORES_AHEAD = 2
ROW_TILE = 256
HALO_ROWS = 16
EDGE = max(POOL_WINDOWS) // 2
MOD_ROWS = 8
V7X_VMEM_BYTES = 64 * 1024 * 1024
VMEM_CAP_BYTES = V7X_VMEM_BYTES - 8 * 1024 * 1024


def _vmem_limit(block_bytes, temp_bytes=0):
    need = 2 * block_bytes + temp_bytes + 4 * 1024 * 1024
    return int(min(max(need, 16 * 1024 * 1024), VMEM_CAP_BYTES))


def _params(semantics, block_bytes, temp_bytes=0):
    return pltpu.CompilerParams(dimension_semantics=semantics,
                                vmem_limit_bytes=_vmem_limit(block_bytes, temp_bytes))


def _largest_tile(n, candidates):
    for c in candidates:
        if n % c == 0:
            return c
    raise ValueError(f"no tile among {candidates} divides {n}")


def _silu(v):
    return v * (1.0 / (1.0 + jnp.exp(-v)))


def _mod_kernel(c_ref, w_ref, b_ref, o_ref):
    a = _silu(c_ref[...]).astype(BF16)
    w = w_ref[0].astype(BF16)
    o_ref[0] = jnp.dot(a, w, preferred_element_type=F32) + b_ref[0]


def _mod_tables(cvec, mod_w, mod_b):
    depth, d, n = mod_w.shape
    tn = _largest_tile(n, (512, 256, 128))
    blocks = MOD_ROWS * d * 4 + d * tn * 4 + tn * 4 + MOD_ROWS * tn * 4
    return pl.pallas_call(
        _mod_kernel,
        grid=(depth, n // tn),
        in_specs=[pl.BlockSpec((MOD_ROWS, d), lambda l, j: (0, 0)),
                  pl.BlockSpec((1, d, tn), lambda l, j: (l, 0, j)),
                  pl.BlockSpec((1, 1, tn), lambda l, j: (l, 0, j))],
        out_specs=pl.BlockSpec((1, MOD_ROWS, tn), lambda l, j: (l, 0, j)),
        out_shape=jax.ShapeDtypeStruct((depth, MOD_ROWS, n), F32),
        compiler_params=_params(("arbitrary", "arbitrary"), blocks, d * tn * 2),
        name="adaln_tables",
    )(cvec, mod_w, mod_b.reshape(depth, 1, n))


def _norm_mod_kernel(x_ref, g_ref, mod_ref, o_ref, *, d, ctx_tiles, ctx_row):
    b, t = pl.program_id(0), pl.program_id(1)
    row = jnp.where(t < ctx_tiles, ctx_row, b)
    shift = mod_ref[pl.ds(row, 1), 0:d]
    scale = mod_ref[pl.ds(row, 1), d:2 * d]
    x = x_ref[...]
    y = x * lax.rsqrt(jnp.mean(x * x, axis=-1, keepdims=True) + NORM_EPS) * g_ref[...]
    o_ref[...] = (y * (1.0 + scale) + shift).astype(o_ref.dtype)


def _norm_mod(xs, g, mods, *, batch, tiles, ctx_tiles):
    r, d = xs.shape
    blocks = ROW_TILE * d * 4 + d * 4 + MOD_ROWS * 3 * d * 4 + ROW_TILE * d * 2
    return pl.pallas_call(
        functools.partial(_norm_mod_kernel, d=d, ctx_tiles=ctx_tiles, ctx_row=batch),
        grid=(batch, tiles),
        in_specs=[pl.BlockSpec((ROW_TILE, d), lambda b, t: (b * tiles + t, 0)),
                  pl.BlockSpec((1, d), lambda b, t: (0, 0)),
                  pl.BlockSpec((MOD_ROWS, 3 * d), lambda b, t: (0, 0))],
        out_specs=pl.BlockSpec((ROW_TILE, d), lambda b, t: (b * tiles + t, 0)),
        out_shape=jax.ShapeDtypeStruct((r, d), BF16),
        compiler_params=_params(("arbitrary", "arbitrary"), blocks, 3 * ROW_TILE * d * 4),
        name="norm_modulate",
    )(xs, g.reshape(1, d), mods)


def _matmul_kernel(a_ref, b_ref, o_ref):
    o_ref[...] = jnp.dot(a_ref[...], b_ref[...].astype(BF16), preferred_element_type=F32).astype(o_ref.dtype)


def _matmul(a, w, layer, *, col0=0, n_cols=None, name="in_proj"):
    m, k = a.shape
    n_cols = w.shape[2] - col0 if n_cols is None else n_cols
    wbytes = jnp.dtype(w.dtype).itemsize
    tn = _largest_tile(math.gcd(n_cols, col0) if col0 else n_cols, (1024, 512, 256, 128) if wbytes == 2 else (512, 256, 128))
    for tm in (1536, 1024, 768, 512, 256):
        blocks = tm * k * 2 + k * tn * wbytes + tm * tn * 2
        temps = tm * tn * (4 + 2) + (k * tn * 2 if wbytes == 4 else 0)
        if m % tm == 0 and 2 * blocks + temps + 4 * 1024 * 1024 <= VMEM_CAP_BYTES:
            break
    else:
        raise ValueError("in-projection blocks do not fit VMEM")
    return pl.pallas_call(
        _matmul_kernel,
        grid=(m // tm, n_cols // tn),
        in_specs=[pl.BlockSpec((tm, k), lambda i, j: (i, 0)),
                  pl.BlockSpec((None, k, tn), lambda i, j: (layer, 0, col0 // tn + j))],
        out_specs=pl.BlockSpec((tm, tn), lambda i, j: (i, j)),
        out_shape=jax.ShapeDtypeStruct((m, n_cols), BF16),
        compiler_params=_params(("arbitrary", "arbitrary"), blocks, temps),
        name=name,
    )(a, w)


def _matmul_res_kernel(a_ref, b_ref, x_ref, gate_ref, o_ref, *, tm, ctx_rows, ctx_row):
    b, t = pl.program_id(0), pl.program_id(1)
    y = jnp.dot(a_ref[...], b_ref[...], preferred_element_type=F32)
    gate_b = gate_ref[pl.ds(b, 1), :]
    gate_c = gate_ref[pl.ds(ctx_row, 1), :]
    rows = t * tm + lax.broadcasted_iota(jnp.int32, (tm, 1), 0)
    gate = jnp.where(rows < ctx_rows, gate_c, gate_b)
    o_ref[...] = x_ref[...] + gate * y


def _matmul_residual(z, w, layer, xs, mods, *, batch, tm, ctx_rows):
    m, kdim = z.shape
    d = w.shape[2]
    for tn in (1024, 512, 256, 128):
        blocks = tm * kdim * 2 + kdim * tn * 2 + 2 * tm * tn * 4 + MOD_ROWS * tn * 4
        if d % tn == 0 and 2 * blocks + 2 * tm * tn * 4 + 4 * 1024 * 1024 <= VMEM_CAP_BYTES:
            break
    else:
        raise ValueError("out-projection blocks do not fit VMEM")
    tiles = m // (batch * tm)
    gate_col0 = 2 * d // tn
    return pl.pallas_call(
        functools.partial(_matmul_res_kernel, tm=tm, ctx_rows=ctx_rows, ctx_row=batch),
        grid=(batch, tiles, d // tn),
        in_specs=[pl.BlockSpec((tm, kdim), lambda b, t, j: (b * tiles + t, 0)),
                  pl.BlockSpec((None, kdim, tn), lambda b, t, j: (layer, 0, j)),
                  pl.BlockSpec((tm, tn), lambda b, t, j: (b * tiles + t, j)),
                  pl.BlockSpec((MOD_ROWS, tn), lambda b, t, j: (0, gate_col0 + j))],
        out_specs=pl.BlockSpec((tm, tn), lambda b, t, j: (b * tiles + t, j)),
        out_shape=jax.ShapeDtypeStruct((m, d), F32),
        input_output_aliases={2: 0},
        compiler_params=_params(("arbitrary",) * 3, blocks, 2 * tm * tn * 4),
        name="out_proj_residual",
    )(z, w, xs, mods)


def _edge_window_sums(x, window):
    half = window // 2
    total = x[EDGE - half:2 * EDGE - half, :]
    for off in range(-half + 1, window - half):
        total = total + x[EDGE + off:2 * EDGE + off, :]
    return total


def _pool_kernel(up_ref, u_ref, un_ref, g_ref, s_ref, o_ref, *,
                 ctx_tiles, tiles, ctx_rows, seq_rows):
    gi, t = pl.program_id(0), pl.program_id(2)
    tr = ROW_TILE
    in_ctx = t < ctx_tiles
    first = (t == 0) | (t == ctx_tiles)
    last = (t == ctx_tiles - 1) | (t == tiles - 1)
    seq_tile = jnp.where(in_ctx, t, t - ctx_tiles)
    n_seq = jnp.where(in_ctx, ctx_rows, seq_rows)

    ub = u_ref[...]
    u = ub.astype(F32)
    above = jnp.where(first, 0.0, up_ref[HALO_ROWS - EDGE:HALO_ROWS, :].astype(F32))
    below = jnp.where(last, 0.0, un_ref[0:EDGE, :].astype(F32))
    top = jnp.concatenate([above, u[0:2 * EDGE, :]], axis=0)
    bottom = jnp.concatenate([u[tr - 2 * EDGE:tr, :], below], axis=0)
    pos = seq_tile * tr + lax.broadcasted_iota(jnp.int32, (tr, 1), 0)
    delta = lax.broadcasted_iota(jnp.int32, (tr, tr), 1) - lax.broadcasted_iota(jnp.int32, (tr, tr), 0)

    for gidx, window in enumerate(POOL_WINDOWS):
        @pl.when(gi == gidx)
        def _(window=window):
            half = window // 2
            band = jnp.where((delta >= -half) & (delta < window - half), 1.0, 0.0).astype(BF16)
            s = jnp.dot(band, ub, preferred_element_type=F32)
            s = jnp.concatenate([_edge_window_sums(top, window), s[EDGE:tr - EDGE, :],
                                 _edge_window_sums(bottom, window)], axis=0)
            cnt = jnp.minimum(pos + (window - half), n_seq) - jnp.maximum(pos - half, 0)
            mixed = s * (1.0 / cnt.astype(F32)) - u
            gate = _silu(g_ref[...].astype(F32))
            o_ref[...] = (mixed * s_ref[...] * gate).astype(o_ref.dtype)


def _fold_kernel(a_ref, b_ref, o_ref, bw_ref):
    @pl.when(pl.program_id(1) == 0)
    def _():
        bw_ref[...] = b_ref[...].astype(BF16)

    o_ref[...] = jnp.dot(a_ref[...].astype(BF16), bw_ref[...], preferred_element_type=F32).astype(o_ref.dtype)


def _fold_group_mix(w_in, w_grp, layer):
    d = w_in.shape[1]
    n_groups, gdim = w_grp.shape[1], w_grp.shape[2]
    tm = _largest_tile(d, (512, 256, 128))
    blocks = tm * gdim * 4 + gdim * gdim * 4 + tm * gdim * 2
    return pl.pallas_call(
        _fold_kernel,
        grid=(n_groups, d // tm),
        in_specs=[pl.BlockSpec((None, tm, gdim), lambda g, i: (layer, i, g)),
                  pl.BlockSpec((None, None, gdim, gdim), lambda g, i: (layer, g, 0, 0))],
        out_specs=pl.BlockSpec((tm, gdim), lambda g, i: (i, g)),
        out_shape=jax.ShapeDtypeStruct((d, n_groups * gdim), BF16),
        scratch_shapes=[pltpu.VMEM((gdim, gdim), BF16)],
        compiler_params=_params(("arbitrary", "arbitrary"), blocks, gdim * gdim * 2 + 2 * tm * gdim * 4),
        name="fold_group_mix",
    )(w_in, w_grp)


def _pool_mix(u, g, scale, *, n_groups, batch, tiles, ctx_tiles):
    r, width = u.shape
    gdim = width // n_groups
    tr = ROW_TILE
    hb = tr // HALO_ROWS
    n_halo_blocks = r // HALO_ROWS
    kern = functools.partial(_pool_kernel, ctx_tiles=ctx_tiles, tiles=tiles,
                             ctx_rows=ctx_tiles * tr, seq_rows=(tiles - ctx_tiles) * tr)
    blocks = 3 * tr * gdim * 2 + 2 * HALO_ROWS * gdim * 2 + gdim * 4
    return pl.pallas_call(
        kern,
        grid=(n_groups, batch, tiles),
        in_specs=[pl.BlockSpec((HALO_ROWS, gdim),
                               lambda c, b, t: (jnp.maximum((b * tiles + t) * hb - 1, 0), c)),
                  pl.BlockSpec((tr, gdim), lambda c, b, t: (b * tiles + t, c)),
                  pl.BlockSpec((HALO_ROWS, gdim),
                               lambda c, b, t: (jnp.minimum((b * tiles + t + 1) * hb, n_halo_blocks - 1), c)),
                  pl.BlockSpec((tr, gdim), lambda c, b, t: (b * tiles + t, c)),
                  pl.BlockSpec((1, gdim), lambda c, b, t: (0, c))],
        out_specs=pl.BlockSpec((tr, gdim), lambda c, b, t: (b * tiles + t, c)),
        out_shape=jax.ShapeDtypeStruct((r, width), BF16),
        compiler_params=_params(("arbitrary",) * 3, blocks, 8 * tr * gdim * 4),
        name="pool_mix",
    )(u, u, u, g, scale.reshape(1, width))


def _rope_tables(n_ctx, n_seq, head_dim):
    rows = n_seq // GRID_W
    row = jnp.repeat(jnp.arange(rows, dtype=F32), GRID_W)
    col = jnp.tile(jnp.arange(GRID_W, dtype=F32), rows)
    n_freq = head_dim // 4
    inv = ROPE_THETA ** (-jnp.arange(n_freq, dtype=F32) / n_freq)
    ang = jnp.concatenate([row[:, None] * inv, col[:, None] * inv], axis=-1)
    ang = jnp.concatenate([jnp.zeros((n_ctx, head_dim // 2), F32), ang], axis=0)
    cos = jnp.repeat(jnp.cos(ang), 2, axis=-1)
    sin = jnp.repeat(jnp.sin(ang), 2, axis=-1)
    sign = jnp.tile(jnp.array([-1.0, 1.0], F32), head_dim // 2)
    return cos, sin * sign


def _pair_swap_matrix(n):
    idx = jnp.arange(n)
    return (idx[:, None] == (idx[None, :] ^ 1)).astype(BF16)


def _swap_pairs(x_bf16, swap_ref):
    return jnp.dot(x_bf16, swap_ref[...], preferred_element_type=F32)


def _qk_prep_kernel(x_ref, cos_ref, sin_ref, g_ref, swap_ref, o_ref, *, q_blocks, heads_per_block, scale):
    j = pl.program_id(2)
    is_q = j < q_blocks
    g = jnp.where(is_q, g_ref[0:1, :], g_ref[2:3, :])
    g_swapped = jnp.where(is_q, g_ref[1:2, :], g_ref[3:4, :])
    out_scale = jnp.where(is_q, scale, 1.0)
    cos, sin = cos_ref[...], sin_ref[...]
    xb = x_ref[...]
    x = xb.astype(F32)
    partner = _swap_pairs(xb, swap_ref)
    hd = ATTN_HEAD_DIM
    outs = []
    for h in range(heads_per_block):
        xh = x[:, h * hd:(h + 1) * hd]
        inv = lax.rsqrt(jnp.mean(xh * xh, axis=-1, keepdims=True) + NORM_EPS) * out_scale
        y = xh * inv * g
        yp = partner[:, h * hd:(h + 1) * hd] * inv * g_swapped
        outs.append(y * cos + yp * sin)
    o_ref[...] = jnp.concatenate(outs, axis=1).astype(o_ref.dtype)


def _qk_prep(proj, cos, sin, q_g, k_g, *, batch, tiles, d, kv_width):
    r = proj.shape[0]
    hd = ATTN_HEAD_DIM
    heads_per_block = 4
    bw = heads_per_block * hd
    width = d + kv_width
    kern = functools.partial(_qk_prep_kernel, q_blocks=d // bw, heads_per_block=heads_per_block,
                             scale=hd ** -0.5 * math.log2(math.e))
    blocks = 2 * ROW_TILE * bw * 2 + 2 * ROW_TILE * hd * 4 + 8 * hd * 4 + bw * bw * 2
    swapped = lambda g: g.reshape(hd // 2, 2)[:, ::-1].reshape(hd)
    gains = jnp.stack([q_g, swapped(q_g), k_g, swapped(k_g)] + [jnp.zeros_like(q_g)] * 4).astype(F32)
    return pl.pallas_call(
        kern,
        grid=(batch, tiles, width // bw),
        in_specs=[pl.BlockSpec((ROW_TILE, bw), lambda b, t, j: (b * tiles + t, j)),
                  pl.BlockSpec((ROW_TILE, hd), lambda b, t, j: (t, 0)),
                  pl.BlockSpec((ROW_TILE, hd), lambda b, t, j: (t, 0)),
                  pl.BlockSpec((8, hd), lambda b, t, j: (0, 0)),
                  pl.BlockSpec((bw, bw), lambda b, t, j: (0, 0))],
        out_specs=pl.BlockSpec((ROW_TILE, bw), lambda b, t, j: (b * tiles + t, j)),
        out_shape=jax.ShapeDtypeStruct((r, width), BF16),
        compiler_params=_params(("arbitrary",) * 3, blocks, 10 * ROW_TILE * bw * 4),
        name="qk_norm_rope",
    )(proj, cos, sin, gains, _pair_swap_matrix(bw))


def _v_transpose_kernel(v_ref, o_ref):
    o_ref[...] = v_ref[...].astype(F32).T.astype(o_ref.dtype)


def _v_transpose(proj, *, batch, n_chunks, kv_chunk, d, kv_width):
    hd = ATTN_HEAD_DIM
    kv_heads = kv_width // hd
    v_col0 = (d + kv_width) // hd
    return pl.pallas_call(
        _v_transpose_kernel,
        grid=(batch, kv_heads, n_chunks),
        in_specs=[pl.BlockSpec((kv_chunk, hd), lambda b, h, c: (b * n_chunks + c, v_col0 + h))],
        out_specs=pl.BlockSpec((None, None, None, hd, kv_chunk), lambda b, h, c: (b, h, c, 0, 0)),
        out_shape=jax.ShapeDtypeStruct((batch, kv_heads, n_chunks, hd, kv_chunk), BF16),
        compiler_params=_params(("arbitrary",) * 3, 2 * kv_chunk * hd * 2, 4 * kv_chunk * hd * 4),
        name="v_transpose",
    )(proj)


def _attn_kernel(q_ref, k_ref, vt_ref, g_ref, o_ref, s_ref, smax_ref, m_ref, l_ref, acc_ref, *,
                 groups, ctx_tiles, ctx_rows, n_chunks, kv_chunk):
    t = pl.program_id(2)
    hd = ATTN_HEAD_DIM
    nt = (((1,), (1,)), ((), ()))

    m_ref[...] = jnp.full(m_ref.shape, -jnp.inf, F32)
    l_ref[...] = jnp.zeros(l_ref.shape, F32)
    acc_ref[...] = jnp.zeros(acc_ref.shape, F32)

    def scores(k, slot, n, h):
        s = lax.dot_general(k, q_ref[:, h * hd:(h + 1) * hd], nt, preferred_element_type=F32)
        s_ref[slot, h, 0:n, :] = s
        smax_ref[slot, h] = jnp.max(s, axis=0, keepdims=True)

    def softmax_pv(vt, slot, n, h):
        s = s_ref[slot, h, 0:n, :]
        m_prev = m_ref[h]
        m_new = jnp.maximum(m_prev, smax_ref[slot, h])
        alpha = jnp.exp2(m_prev - m_new)
        p = jnp.exp2(s - m_new)
        l_ref[h] = alpha * l_ref[h] + jnp.sum(p, axis=0, keepdims=True)
        m_ref[h] = m_new
        acc_ref[h] = alpha * acc_ref[h] + jnp.dot(vt, p.astype(BF16), preferred_element_type=F32)

    def chunk_step(k_next, vt, slot, n):
        ahead = min(ATTN_SCORES_AHEAD, groups)
        if k_next is not None:
            for h in range(ahead):
                scores(k_next, 1 - slot, n, h)
        for h in range(groups):
            softmax_pv(vt, slot, n, h)
            if k_next is not None and h + ahead < groups:
                scores(k_next, 1 - slot, n, h + ahead)

    def k_chunk(c):
        return k_ref[pl.ds(pl.multiple_of(c * kv_chunk, kv_chunk), kv_chunk), :]

    @pl.when(t < ctx_tiles)
    def _():
        for r0 in range(0, ctx_rows, ROW_TILE):
            c, off = divmod(r0, kv_chunk)
            for h in range(groups):
                scores(k_ref[r0:r0 + ROW_TILE, :], 0, ROW_TILE, h)
            chunk_step(None, vt_ref[c, :, off:off + ROW_TILE], 0, ROW_TILE)

    @pl.when(t >= ctx_tiles)
    def _():
        for h in range(groups):
            scores(k_chunk(0), 0, kv_chunk, h)
        pairs = (n_chunks - 1) // 2

        def body(i, carry):
            c = 2 * i
            chunk_step(k_chunk(c + 1), vt_ref[c], 0, kv_chunk)
            chunk_step(k_chunk(c + 2), vt_ref[c + 1], 1, kv_chunk)
            return carry
        lax.fori_loop(0, pairs, body, 0)
        if n_chunks % 2 == 0:
            chunk_step(k_chunk(n_chunks - 1), vt_ref[n_chunks - 2], 0, kv_chunk)
            chunk_step(None, vt_ref[n_chunks - 1], 1, kv_chunk)
        else:
            chunk_step(None, vt_ref[n_chunks - 1], 0, kv_chunk)

    o = jnp.concatenate([(acc_ref[h] * (1.0 / l_ref[h])).T for h in range(groups)], axis=1)
    o_ref[...] = (o * _silu(g_ref[...].astype(F32))).astype(o_ref.dtype)


def _attention(qk, proj, *, batch, tiles, ctx_tiles, d, kv_width):
    r = qk.shape[0]
    hd = ATTN_HEAD_DIM
    kv_heads = kv_width // hd
    groups = d // kv_width
    bw = groups * hd
    kv_rows = tiles * ROW_TILE
    kv_chunk = _largest_tile(kv_rows, (768, 512, 256))
    n_chunks = kv_rows // kv_chunk
    vt = _v_transpose(proj, batch=batch, n_chunks=n_chunks, kv_chunk=kv_chunk, d=d, kv_width=kv_width)
    kern = functools.partial(_attn_kernel, groups=groups, ctx_tiles=ctx_tiles,
                             ctx_rows=ctx_tiles * ROW_TILE, n_chunks=n_chunks, kv_chunk=kv_chunk)
    blocks = 3 * ROW_TILE * bw * 2 + 2 * kv_rows * hd * 2
    temps = (2 * groups + 6) * kv_chunk * ROW_TILE * 4 + groups * (hd + 16) * ROW_TILE * 4
    return pl.pallas_call(
        kern,
        grid=(batch, kv_heads, tiles),
        in_specs=[pl.BlockSpec((ROW_TILE, bw), lambda b, h, t: (b * tiles + t, h)),
                  pl.BlockSpec((kv_rows, hd), lambda b, h, t: (b, d // hd + h)),
                  pl.BlockSpec((None, None, n_chunks, hd, kv_chunk), lambda b, h, t: (b, h, 0, 0, 0)),
                  pl.BlockSpec((ROW_TILE, bw), lambda b, h, t: (b * tiles + t, (d + 2 * kv_width) // bw + h))],
        out_specs=pl.BlockSpec((ROW_TILE, bw), lambda b, h, t: (b * tiles + t, h)),
        out_shape=jax.ShapeDtypeStruct((r, d), BF16),
        scratch_shapes=[pltpu.VMEM((2, groups, kv_chunk, ROW_TILE), F32),
                        pltpu.VMEM((2, groups, 1, ROW_TILE), F32),
                        pltpu.VMEM((groups, 1, ROW_TILE), F32),
                        pltpu.VMEM((groups, 1, ROW_TILE), F32),
                        pltpu.VMEM((groups, hd, ROW_TILE), F32)],
        compiler_params=_params(("arbitrary",) * 3, blocks, temps),
        name="attention",
    )(qk, qk, vt, proj)


def _ret_chunk(q_ref, k_ref, v_ref, cos_ref, sin_ref, dec_ref, swap_ref, s_ref, *, qk_dim, hps, reverse):
    tc = ROW_TILE
    v_dim = 2 * qk_dim
    nt = (((1,), (1,)), ((), ()))
    cos, sin = cos_ref[...], sin_ref[...]
    ii = lax.broadcasted_iota(jnp.int32, (tc, tc), 0)
    jj = lax.broadcasted_iota(jnp.int32, (tc, tc), 1)
    pos = lax.broadcasted_iota(jnp.int32, (tc, 1), 0).astype(F32)
    if reverse:
        dist, keep = (jj - ii), jj > ii
        q_pow, k_pow = tc - pos, pos
    else:
        dist, keep = (ii - jj), ii >= jj
        q_pow, k_pow = pos + 1.0, tc - 1.0 - pos
    dist = jnp.maximum(dist, 0).astype(F32)

    raw = []
    for h in range(hps):
        qb = q_ref[:, h * qk_dim:(h + 1) * qk_dim]
        kb = k_ref[:, h * qk_dim:(h + 1) * qk_dim]
        raw.append((qb, kb, _swap_pairs(qb, swap_ref), _swap_pairs(kb, swap_ref)))

    heads = []
    for h, (qb, kb, q_partner, k_partner) in enumerate(raw):
        log_g = jnp.log1p(-jnp.exp(dec_ref[h]))[0:1, 0:1]
        q = qb.astype(F32) * cos + q_partner * sin
        k = (kb.astype(F32) * cos + k_partner * sin) * (qk_dim ** -0.5)
        scores = lax.dot_general(q.astype(BF16), k.astype(BF16), nt, preferred_element_type=F32)
        heads.append((log_g, q, k, scores))

    outs = []
    for h, (log_g, q, k, scores) in enumerate(heads):
        v = v_ref[:, h * v_dim:(h + 1) * v_dim]
        decay = jnp.where(keep, jnp.exp(dist * log_g), 0.0)
        lhs = jnp.concatenate([(scores * decay).astype(BF16),
                               (q * jnp.exp(q_pow * log_g)).astype(BF16)], axis=1)
        rhs = jnp.concatenate([v, s_ref[h].astype(BF16)], axis=0)
        outs.append(jnp.dot(lhs, rhs, preferred_element_type=F32))

    for h, (log_g, q, k, scores) in enumerate(heads):
        v = v_ref[:, h * v_dim:(h + 1) * v_dim]
        kt = (k * jnp.exp(k_pow * log_g)).T.astype(BF16)
        s_ref[h] = s_ref[h] * jnp.exp(tc * log_g) + jnp.dot(kt, v, preferred_element_type=F32)
    return outs


def _ret_fwd_kernel(q_ref, k_ref, v_ref, cos_ref, sin_ref, dec_ref, swap_ref, o_ref, s_ref, *, qk_dim, hps):
    @pl.when(pl.program_id(2) == 0)
    def _():
        s_ref[...] = jnp.zeros(s_ref.shape, F32)

    outs = _ret_chunk(q_ref, k_ref, v_ref, cos_ref, sin_ref, dec_ref, swap_ref, s_ref,
                      qk_dim=qk_dim, hps=hps, reverse=False)
    o_ref[...] = jnp.concatenate(outs, axis=1).astype(o_ref.dtype)


def _ret_bwd_kernel(q_ref, k_ref, v_ref, cos_ref, sin_ref, dec_ref, swap_ref, of_ref, g_ref, gn_ref, o_ref, s_ref,
                    *, qk_dim, hps):
    @pl.when(pl.program_id(2) == 0)
    def _():
        s_ref[...] = jnp.zeros(s_ref.shape, F32)

    outs = _ret_chunk(q_ref, k_ref, v_ref, cos_ref, sin_ref, dec_ref, swap_ref, s_ref,
                      qk_dim=qk_dim, hps=hps, reverse=True)
    v_dim = 2 * qk_dim
    normed = []
    for h, o in enumerate(outs):
        cols = slice(h * v_dim, (h + 1) * v_dim)
        o = o + of_ref[:, cols].astype(F32)
        mu = jnp.mean(o, axis=-1, keepdims=True)
        dev = o - mu
        var = jnp.mean(dev * dev, axis=-1, keepdims=True)
        normed.append(dev * lax.rsqrt(var + GN_EPS) * gn_ref[:, cols])
    o_ref[...] = (jnp.concatenate(normed, axis=1) * _silu(g_ref[...].astype(F32))).astype(o_ref.dtype)


def _retention(proj, cos, sin, decay_f, decay_b, gn_g, *, batch, tiles, ctx_tiles, d, heads):
    r = proj.shape[0]
    qk_dim = d // heads
    v_dim = 2 * qk_dim
    tc = ROW_TILE
    dec_f = jnp.broadcast_to(decay_f.astype(F32)[:, None, None], (heads, 8, 128))
    dec_b = jnp.broadcast_to(decay_b.astype(F32)[:, None, None], (heads, 8, 128))
    hps = _largest_tile(heads, (RET_HEADS_PER_STEP, 1))
    swap = _pair_swap_matrix(qk_dim)
    qw, vw = hps * qk_dim, hps * v_dim
    k_col0 = d // qw
    v_col0 = 2 * d // vw
    g_col0 = (2 * d + heads * v_dim) // vw
    blocks = 2 * tc * qw * 2 + 4 * tc * vw * 2 + 2 * tc * qk_dim * 4 + hps * 8 * 128 * 4 + vw * 4
    temps = hps * (qk_dim * v_dim * 4 + 12 * tc * v_dim * 4)

    def fwd_row(b, h, c):
        return b * tiles + c

    def bwd_chunk(c):
        return jnp.where(c < ctx_tiles, ctx_tiles - 1 - c, tiles - 1 - (c - ctx_tiles))

    def bwd_row(b, h, c):
        return b * tiles + bwd_chunk(c)

    o_f = pl.pallas_call(
        functools.partial(_ret_fwd_kernel, qk_dim=qk_dim, hps=hps),
        grid=(batch, heads // hps, tiles),
        in_specs=[pl.BlockSpec((tc, qw), lambda b, h, c: (fwd_row(b, h, c), h)),
                  pl.BlockSpec((tc, qw), lambda b, h, c: (fwd_row(b, h, c), k_col0 + h)),
                  pl.BlockSpec((tc, vw), lambda b, h, c: (fwd_row(b, h, c), v_col0 + h)),
                  pl.BlockSpec((tc, qk_dim), lambda b, h, c: (c, 0)),
                  pl.BlockSpec((tc, qk_dim), lambda b, h, c: (c, 0)),
                  pl.BlockSpec((hps, 8, 128), lambda b, h, c: (h, 0, 0)),
                  pl.BlockSpec((qk_dim, qk_dim), lambda b, h, c: (0, 0))],
        out_specs=pl.BlockSpec((tc, vw), lambda b, h, c: (fwd_row(b, h, c), h)),
        out_shape=jax.ShapeDtypeStruct((r, heads * v_dim), BF16),
        scratch_shapes=[pltpu.VMEM((hps, qk_dim, v_dim), F32)],
        compiler_params=_params(("arbitrary",) * 3, blocks, temps),
        name="retention_fwd",
    )(proj, proj, proj, cos, sin, dec_f, swap)

    return pl.pallas_call(
        functools.partial(_ret_bwd_kernel, qk_dim=qk_dim, hps=hps),
        grid=(batch, heads // hps, tiles),
        in_specs=[pl.BlockSpec((tc, qw), lambda b, h, c: (bwd_row(b, h, c), h)),
                  pl.BlockSpec((tc, qw), lambda b, h, c: (bwd_row(b, h, c), k_col0 + h)),
                  pl.BlockSpec((tc, vw), lambda b, h, c: (bwd_row(b, h, c), v_col0 + h)),
                  pl.BlockSpec((tc, qk_dim), lambda b, h, c: (bwd_chunk(c), 0)),
                  pl.BlockSpec((tc, qk_dim), lambda b, h, c: (bwd_chunk(c), 0)),
                  pl.BlockSpec((hps, 8, 128), lambda b, h, c: (h, 0, 0)),
                  pl.BlockSpec((qk_dim, qk_dim), lambda b, h, c: (0, 0)),
                  pl.BlockSpec((tc, vw), lambda b, h, c: (bwd_row(b, h, c), h)),
                  pl.BlockSpec((tc, vw), lambda b, h, c: (bwd_row(b, h, c), g_col0 + h)),
                  pl.BlockSpec((1, vw), lambda b, h, c: (0, h))],
        out_specs=pl.BlockSpec((tc, vw), lambda b, h, c: (bwd_row(b, h, c), h)),
        out_shape=jax.ShapeDtypeStruct((r, heads * v_dim), BF16),
        scratch_shapes=[pltpu.VMEM((hps, qk_dim, v_dim), F32)],
        compiler_params=_params(("arbitrary",) * 3, blocks, temps),
        name="retention_bwd_norm_gate",
    )(proj, proj, proj, cos, sin, dec_b, swap, o_f, proj, gn_g.astype(F32).reshape(1, heads * v_dim))


def _final_norm_kernel(x_ref, g_ref, o_ref):
    x = x_ref[...]
    o_ref[...] = x * lax.rsqrt(jnp.mean(x * x, axis=-1, keepdims=True) + NORM_EPS) * g_ref[...]


def _final_norm(xs, g, *, batch, tiles, ctx_tiles):
    d = xs.shape[1]
    seq_tiles = tiles - ctx_tiles
    blocks = 2 * ROW_TILE * d * 4 + d * 4
    return pl.pallas_call(
        _final_norm_kernel,
        grid=(batch, seq_tiles),
        in_specs=[pl.BlockSpec((ROW_TILE, d), lambda b, t: (b * tiles + ctx_tiles + t, 0)),
                  pl.BlockSpec((1, d), lambda b, t: (0, 0))],
        out_specs=pl.BlockSpec((ROW_TILE, d), lambda b, t: (b * seq_tiles + t, 0)),
        out_shape=jax.ShapeDtypeStruct((batch * seq_tiles * ROW_TILE, d), F32),
        compiler_params=_params(("arbitrary", "arbitrary"), blocks, 2 * ROW_TILE * d * 4),
        name="final_norm",
    )(xs, g.reshape(1, d))


def kernel(x, c, ctx, c_ctx, norm_g, mod_w, mod_b, pool_w_in, pool_w_grp, pool_scale, pool_w_out,
           attn_w_in, attn_q_norm, attn_k_norm, attn_w_out, ret_w_in, ret_decay_fwd, ret_decay_bwd,
           ret_gn_g, ret_w_out, final_norm_g):
    batch, seq, d = x.shape
    n_ctx = ctx.shape[1]
    depth = norm_g.shape[0]
    assert seq % ROW_TILE == 0 and n_ctx % ROW_TILE == 0 and seq % GRID_W == 0
    assert batch < MOD_ROWS
    ctx_tiles = n_ctx // ROW_TILE
    tiles = (n_ctx + seq) // ROW_TILE
    rows_per_sample = n_ctx + seq
    tm = _largest_tile(rows_per_sample, (768, 512, 256))

    xs = jnp.concatenate([ctx, x], axis=1).reshape(batch * rows_per_sample, d)
    cvec = jnp.concatenate([c, c_ctx[None, :], jnp.zeros((MOD_ROWS - batch - 1, d), F32)], axis=0)
    mods = _mod_tables(cvec, mod_w, mod_b)
    geom = dict(batch=batch, tiles=tiles, ctx_tiles=ctx_tiles)

    kv_width = (attn_w_in.shape[2] - 2 * d) // 2
    ret_heads = ret_decay_fwd.shape[1]
    attn_cos, attn_sin = _rope_tables(n_ctx, seq, ATTN_HEAD_DIM)
    ret_cos, ret_sin = _rope_tables(n_ctx, seq, d // ret_heads)

    pool_w_out_b, attn_w_out_b, ret_w_out_b = (w.astype(BF16) for w in (pool_w_out, attn_w_out, ret_w_out))
    for i in range(depth):
        kind, j = i % 3, i // 3
        h = _norm_mod(xs, norm_g[i], mods[i], **geom)
        if kind == 0:
            width = pool_w_in.shape[2] // 2
            u = _matmul(h, _fold_group_mix(pool_w_in, pool_w_grp, j)[None], 0, name="in_proj_mixed")
            g = _matmul(h, pool_w_in, j, col0=width, n_cols=width)
            z = _pool_mix(u, g, pool_scale[j], n_groups=pool_w_grp.shape[1], **geom)
            w_out = pool_w_out_b
        elif kind == 1:
            proj = _matmul(h, attn_w_in, j)
            qk = _qk_prep(proj, attn_cos, attn_sin, attn_q_norm[j], attn_k_norm[j],
                          batch=batch, tiles=tiles, d=d, kv_width=kv_width)
            z = _attention(qk, proj, **geom, d=d, kv_width=kv_width)
            w_out = attn_w_out_b
        else:
            proj = _matmul(h, ret_w_in, j)
            z = _retention(proj, ret_cos, ret_sin, ret_decay_fwd[j], ret_decay_bwd[j], ret_gn_g[j],
                           **geom, d=d, heads=ret_heads)
            w_out = ret_w_out_b
        xs = _matmul_residual(z, w_out, j, xs, mods[i], batch=batch, tm=tm, ctx_rows=n_ctx)

    out = _final_norm(xs, final_norm_g, **geom)
    return out.reshape(batch, seq, d)
```

```python
import functools
import math

import jax
import jax.numpy as jnp
from jax import lax
from jax.experimental import pallas as pl
from jax.experimental.pallas import tpu as pltpu

F32 = jnp.float32
BF16 = jnp.bfloat16

GRID_W = 64
ATTN_HEAD_DIM = 128
POOL_WINDOWS = (2, 4, 8, 16)
ROPE_THETA = 10000.0
NORM_EPS = 1e-6
GN_EPS = 1e-5

RET_HEADS_PER_STEP = 4
ATTN_SCORES_AHEAD = 2
ROW_TILE = 256
HALO_ROWS = 16
EDGE = max(POOL_WINDOWS) // 2
MOD_ROWS = 8
V7X_VMEM_BYTES = 64 * 1024 * 1024
VMEM_CAP_BYTES = V7X_VMEM_BYTES - 8 * 1024 * 1024


def _vmem_limit(block_bytes, temp_bytes=0):
    need = 2 * block_bytes + temp_bytes + 4 * 1024 * 1024
    return int(min(max(need, 16 * 1024 * 1024), VMEM_CAP_BYTES))


def _params(semantics, block_bytes, temp_bytes=0):
    return pltpu.CompilerParams(dimension_semantics=semantics,
                                vmem_limit_bytes=_vmem_limit(block_bytes, temp_bytes))


def _largest_tile(n, candidates):
    for c in candidates:
        if n % c == 0:
            return c
    raise ValueError(f"no tile among {candidates} divides {n}")


def _silu(v):
    return v * (1.0 / (1.0 + jnp.exp(-v)))


def _mod_kernel(c_ref, w_ref, b_ref, o_ref):
    a = _silu(c_ref[...]).astype(BF16)
    w = w_ref[0].astype(BF16)
    o_ref[0] = jnp.dot(a, w, preferred_element_type=F32) + b_ref[0]


def _mod_tables(cvec, mod_w, mod_b):
    depth, d, n = mod_w.shape
    tn = _largest_tile(n, (512, 256, 128))
    blocks = MOD_ROWS * d * 4 + d * tn * 4 + tn * 4 + MOD_ROWS * tn * 4
    return pl.pallas_call(
        _mod_kernel,
        grid=(depth, n // tn),
        in_specs=[pl.BlockSpec((MOD_ROWS, d), lambda l, j: (0, 0)),
                  pl.BlockSpec((1, d, tn), lambda l, j: (l, 0, j)),
                  pl.BlockSpec((1, 1, tn), lambda l, j: (l, 0, j))],
        out_specs=pl.BlockSpec((1, MOD_ROWS, tn), lambda l, j: (l, 0, j)),
        out_shape=jax.ShapeDtypeStruct((depth, MOD_ROWS, n), F32),
        compiler_params=_params(("arbitrary", "arbitrary"), blocks, d * tn * 2),
        name="adaln_tables",
    )(cvec, mod_w, mod_b.reshape(depth, 1, n))


def _norm_mod_kernel(x_ref, g_ref, mod_ref, o_ref, *, d, ctx_tiles, ctx_row):
    b, t = pl.program_id(0), pl.program_id(1)
    row = jnp.where(t < ctx_tiles, ctx_row, b)
    shift = mod_ref[pl.ds(row, 1), 0:d]
    scale = mod_ref[pl.ds(row, 1), d:2 * d]
    x = x_ref[...]
    y = x * lax.rsqrt(jnp.mean(x * x, axis=-1, keepdims=True) + NORM_EPS) * g_ref[...]
    o_ref[...] = (y * (1.0 + scale) + shift).astype(o_ref.dtype)


def _norm_mod(xs, g, mods, *, batch, tiles, ctx_tiles):
    r, d = xs.shape
    blocks = ROW_TILE * d * 4 + d * 4 + MOD_ROWS * 3 * d * 4 + ROW_TILE * d * 2
    return pl.pallas_call(
        functools.partial(_norm_mod_kernel, d=d, ctx_tiles=ctx_tiles, ctx_row=batch),
        grid=(batch, tiles),
        in_specs=[pl.BlockSpec((ROW_TILE, d), lambda b, t: (b * tiles + t, 0)),
                  pl.BlockSpec((1, d), lambda b, t: (0, 0)),
                  pl.BlockSpec((MOD_ROWS, 3 * d), lambda b, t: (0, 0))],
        out_specs=pl.BlockSpec((ROW_TILE, d), lambda b, t: (b * tiles + t, 0)),
        out_shape=jax.ShapeDtypeStruct((r, d), BF16),
        compiler_params=_params(("arbitrary", "arbitrary"), blocks, 3 * ROW_TILE * d * 4),
        name="norm_modulate",
    )(xs, g.reshape(1, d), mods)


def _matmul_kernel(a_ref, b_ref, o_ref):
    o_ref[...] = jnp.dot(a_ref[...], b_ref[...].astype(BF16), preferred_element_type=F32).astype(o_ref.dtype)


def _matmul(a, w, layer, *, col0=0, n_cols=None, name="in_proj"):
    m, k = a.shape
    n_cols = w.shape[2] - col0 if n_cols is None else n_cols
    wbytes = jnp.dtype(w.dtype).itemsize
    tn = _largest_tile(math.gcd(n_cols, col0) if col0 else n_cols, (1024, 512, 256, 128) if wbytes == 2 else (512, 256, 128))
    for tm in (1536, 1024, 768, 512, 256):
        blocks = tm * k * 2 + k * tn * wbytes + tm * tn * 2
        temps = tm * tn * (4 + 2) + (k * tn * 2 if wbytes == 4 else 0)
        if m % tm == 0 and 2 * blocks + temps + 4 * 1024 * 1024 <= VMEM_CAP_BYTES:
            break
    else:
        raise ValueError("in-projection blocks do not fit VMEM")
    return pl.pallas_call(
        _matmul_kernel,
        grid=(m // tm, n_cols // tn),
        in_specs=[pl.BlockSpec((tm, k), lambda i, j: (i, 0)),
                  pl.BlockSpec((None, k, tn), lambda i, j: (layer, 0, col0 // tn + j))],
        out_specs=pl.BlockSpec((tm, tn), lambda i, j: (i, j)),
        out_shape=jax.ShapeDtypeStruct((m, n_cols), BF16),
        compiler_params=_params(("arbitrary", "arbitrary"), blocks, temps),
        name=name,
    )(a, w)


def _matmul_res_kernel(a_ref, b_ref, x_ref, gate_ref, o_ref, *, tm, ctx_rows, ctx_row):
    b, t = pl.program_id(0), pl.program_id(1)
    y = jnp.dot(a_ref[...], b_ref[...], preferred_element_type=F32)
    gate_b = gate_ref[pl.ds(b, 1), :]
    gate_c = gate_ref[pl.ds(ctx_row, 1), :]
    rows = t * tm + lax.broadcasted_iota(jnp.int32, (tm, 1), 0)
    gate = jnp.where(rows < ctx_rows, gate_c, gate_b)
    o_ref[...] = x_ref[...] + gate * y


def _matmul_residual(z, w, layer, xs, mods, *, batch, tm, ctx_rows):
    m, kdim = z.shape
    d = w.shape[2]
    for tn in (1024, 512, 256, 128):
        blocks = tm * kdim * 2 + kdim * tn * 2 + 2 * tm * tn * 4 + MOD_ROWS * tn * 4
        if d % tn == 0 and 2 * blocks + 2 * tm * tn * 4 + 4 * 1024 * 1024 <= VMEM_CAP_BYTES:
            break
    else:
        raise ValueError("out-projection blocks do not fit VMEM")
    tiles = m // (batch * tm)
    gate_col0 = 2 * d // tn
    return pl.pallas_call(
        functools.partial(_matmul_res_kernel, tm=tm, ctx_rows=ctx_rows, ctx_row=batch),
        grid=(batch, tiles, d // tn),
        in_specs=[pl.BlockSpec((tm, kdim), lambda b, t, j: (b * tiles + t, 0)),
                  pl.BlockSpec((None, kdim, tn), lambda b, t, j: (layer, 0, j)),
                  pl.BlockSpec((tm, tn), lambda b, t, j: (b * tiles + t, j)),
                  pl.BlockSpec((MOD_ROWS, tn), lambda b, t, j: (0, gate_col0 + j))],
        out_specs=pl.BlockSpec((tm, tn), lambda b, t, j: (b * tiles + t, j)),
        out_shape=jax.ShapeDtypeStruct((m, d), F32),
        input_output_aliases={2: 0},
        compiler_params=_params(("arbitrary",) * 3, blocks, 2 * tm * tn * 4),
        name="out_proj_residual",
    )(z, w, xs, mods)


def _edge_window_sums(x, window):
    half = window // 2
    total = x[EDGE - half:2 * EDGE - half, :]
    for off in range(-half + 1, window - half):
        total = total + x[EDGE + off:2 * EDGE + off, :]
    return total


def _pool_kernel(up_ref, u_ref, un_ref, g_ref, o_ref, *,
                 ctx_tiles, tiles, ctx_rows, seq_rows):
    t = pl.program_id(1)
    tr = ROW_TILE
    gdim = u_ref.shape[1] // len(POOL_WINDOWS)
    in_ctx = t < ctx_tiles
    first = (t == 0) | (t == ctx_tiles)
    last = (t == ctx_tiles - 1) | (t == tiles - 1)
    seq_tile = jnp.where(in_ctx, t, t - ctx_tiles)
    n_seq = jnp.where(in_ctx, ctx_rows, seq_rows)
    pos = seq_tile * tr + lax.broadcasted_iota(jnp.int32, (tr, 1), 0)
    delta = lax.broadcasted_iota(jnp.int32, (tr, tr), 1) - lax.broadcasted_iota(jnp.int32, (tr, tr), 0)

    for gidx, window in enumerate(POOL_WINDOWS):
        cols = slice(gidx * gdim, (gidx + 1) * gdim)
        half = window // 2
        ub = u_ref[:, cols]
        above = jnp.where(first, 0.0, up_ref[HALO_ROWS - EDGE:HALO_ROWS, cols].astype(F32))
        below = jnp.where(last, 0.0, un_ref[0:EDGE, cols].astype(F32))
        u_head = ub[0:2 * EDGE, :].astype(F32)
        u_tail = ub[tr - 2 * EDGE:tr, :].astype(F32)
        band = (jnp.where((delta >= -half) & (delta < window - half), 1.0 / window, 0.0)
                - jnp.where(delta == 0, 1.0, 0.0)).astype(BF16)
        inner = jnp.dot(band, ub, preferred_element_type=F32)
        cnt = jnp.minimum(pos + (window - half), n_seq) - jnp.maximum(pos - half, 0)
        inv = 1.0 / cnt.astype(F32)
        head = (_edge_window_sums(jnp.concatenate([above, u_head], axis=0), window) * inv[0:EDGE, :]
                - u_head[0:EDGE, :])
        tail = (_edge_window_sums(jnp.concatenate([u_tail, below], axis=0), window) * inv[tr - EDGE:tr, :]
                - u_tail[EDGE:2 * EDGE, :])
        mixed = jnp.concatenate([head, inner[EDGE:tr - EDGE, :], tail], axis=0)
        o_ref[:, cols] = (mixed * _silu(g_ref[:, cols].astype(F32))).astype(o_ref.dtype)


def _fold_kernel(a_ref, b_ref, scale_ref, o_ref, bw_ref):
    @pl.when(pl.program_id(1) == 0)
    def _():
        bw_ref[...] = b_ref[...].astype(BF16)

    mixed = jnp.dot(a_ref[...].astype(BF16), bw_ref[...], preferred_element_type=F32)
    o_ref[...] = (mixed * scale_ref[...]).astype(o_ref.dtype)


def _fold_group_mix(w_in, w_grp, scale, layer):
    d = w_in.shape[1]
    n_groups, gdim = w_grp.shape[1], w_grp.shape[2]
    tm = _largest_tile(d, (512, 256, 128))
    blocks = tm * gdim * 4 + gdim * gdim * 4 + tm * gdim * 2
    return pl.pallas_call(
        _fold_kernel,
        grid=(n_groups, d // tm),
        in_specs=[pl.BlockSpec((None, tm, gdim), lambda g, i: (layer, i, g)),
                  pl.BlockSpec((None, None, gdim, gdim), lambda g, i: (layer, g, 0, 0)),
                  pl.BlockSpec((None, 1, gdim), lambda g, i: (layer, 0, g))],
        out_specs=pl.BlockSpec((tm, gdim), lambda g, i: (i, g)),
        out_shape=jax.ShapeDtypeStruct((d, n_groups * gdim), BF16),
        scratch_shapes=[pltpu.VMEM((gdim, gdim), BF16)],
        compiler_params=_params(("arbitrary", "arbitrary"), blocks, gdim * gdim * 2 + 2 * tm * gdim * 4),
        name="fold_group_mix",
    )(w_in, w_grp, scale.reshape(scale.shape[0], 1, scale.shape[1]))


def _pool_mix(u, g, *, n_groups, batch, tiles, ctx_tiles):
    r, width = u.shape
    gdim = width // n_groups
    tr = ROW_TILE
    hb = tr // HALO_ROWS
    n_halo_blocks = r // HALO_ROWS
    kern = functools.partial(_pool_kernel, ctx_tiles=ctx_tiles, tiles=tiles,
                             ctx_rows=ctx_tiles * tr, seq_rows=(tiles - ctx_tiles) * tr)
    assert n_groups == len(POOL_WINDOWS)
    blocks = 3 * tr * width * 2 + 2 * HALO_ROWS * width * 2
    return pl.pallas_call(
        kern,
        grid=(batch, tiles),
        in_specs=[pl.BlockSpec((HALO_ROWS, width), lambda b, t: (jnp.maximum((b * tiles + t) * hb - 1, 0), 0)),
                  pl.BlockSpec((tr, width), lambda b, t: (b * tiles + t, 0)),
                  pl.BlockSpec((HALO_ROWS, width),
                               lambda b, t: (jnp.minimum((b * tiles + t + 1) * hb, n_halo_blocks - 1), 0)),
                  pl.BlockSpec((tr, width), lambda b, t: (b * tiles + t, 0))],
        out_specs=pl.BlockSpec((tr, width), lambda b, t: (b * tiles + t, 0)),
        out_shape=jax.ShapeDtypeStruct((r, width), BF16),
        compiler_params=_params(("arbitrary",) * 2, blocks, 10 * tr * gdim * 4),
        name="pool_mix",
    )(u, u, u, g)


def _rope_tables(n_ctx, n_seq, head_dim):
    rows = n_seq // GRID_W
    row = jnp.repeat(jnp.arange(rows, dtype=F32), GRID_W)
    col = jnp.tile(jnp.arange(GRID_W, dtype=F32), rows)
    n_freq = head_dim // 4
    inv = ROPE_THETA ** (-jnp.arange(n_freq, dtype=F32) / n_freq)
    ang = jnp.concatenate([row[:, None] * inv, col[:, None] * inv], axis=-1)
    ang = jnp.concatenate([jnp.zeros((n_ctx, head_dim // 2), F32), ang], axis=0)
    cos = jnp.repeat(jnp.cos(ang), 2, axis=-1)
    sin = jnp.repeat(jnp.sin(ang), 2, axis=-1)
    sign = jnp.tile(jnp.array([-1.0, 1.0], F32), head_dim // 2)
    return cos, sin * sign


def _pair_swap_matrix(n):
    idx = jnp.arange(n)
    return (idx[:, None] == (idx[None, :] ^ 1)).astype(BF16)


def _swap_pairs(x_bf16, swap_ref):
    return jnp.dot(x_bf16, swap_ref[...], preferred_element_type=F32)


def _norm_rope(xb, cos, sin, gain, gain_swapped, swap_ref, out_scale):
    hd = ATTN_HEAD_DIM
    x = xb.astype(F32)
    partner = _swap_pairs(xb, swap_ref)
    outs = []
    for h in range(xb.shape[1] // hd):
        cols = slice(h * hd, (h + 1) * hd)
        xh = x[:, cols]
        inv = lax.rsqrt(jnp.mean(xh * xh, axis=-1, keepdims=True) + NORM_EPS) * out_scale
        outs.append(xh * inv * gain * cos + partner[:, cols] * inv * gain_swapped * sin)
    return outs[0] if len(outs) == 1 else jnp.concatenate(outs, axis=1)


def _gain_rows(g):
    hd = ATTN_HEAD_DIM
    swapped = g.reshape(hd // 2, 2)[:, ::-1].reshape(hd)
    return jnp.concatenate([g[None], swapped[None], jnp.zeros((MOD_ROWS - 2, hd), g.dtype)], axis=0).astype(F32)


def _kv_prep_kernel(k_ref, v_ref, cos_ref, sin_ref, g_ref, swap_ref, ko_ref, vt_ref):
    k = _norm_rope(k_ref[...], cos_ref[...], sin_ref[...], g_ref[0:1, :], g_ref[1:2, :], swap_ref, 1.0)
    ko_ref[...] = k.astype(ko_ref.dtype)
    vt_ref[...] = v_ref[...].astype(F32).T.astype(vt_ref.dtype)


def _kv_prep(proj, cos, sin, k_g, *, batch, n_chunks, kv_chunk, d, kv_width):
    hd = ATTN_HEAD_DIM
    kv_heads = kv_width // hd
    k_col0 = d // hd
    v_col0 = (d + kv_width) // hd
    blocks = 4 * kv_chunk * hd * 2 + 2 * kv_chunk * hd * 4 + MOD_ROWS * hd * 4 + hd * hd * 2
    return pl.pallas_call(
        _kv_prep_kernel,
        grid=(batch, kv_heads, n_chunks),
        in_specs=[pl.BlockSpec((kv_chunk, hd), lambda b, h, c: (b * n_chunks + c, k_col0 + h)),
                  pl.BlockSpec((kv_chunk, hd), lambda b, h, c: (b * n_chunks + c, v_col0 + h)),
                  pl.BlockSpec((kv_chunk, hd), lambda b, h, c: (c, 0)),
                  pl.BlockSpec((kv_chunk, hd), lambda b, h, c: (c, 0)),
                  pl.BlockSpec((MOD_ROWS, hd), lambda b, h, c: (0, 0)),
                  pl.BlockSpec((hd, hd), lambda b, h, c: (0, 0))],
        out_specs=[pl.BlockSpec((kv_chunk, hd), lambda b, h, c: (b * n_chunks + c, h)),
                   pl.BlockSpec((None, None, None, hd, kv_chunk), lambda b, h, c: (b, h, c, 0, 0))],
        out_shape=[jax.ShapeDtypeStruct((proj.shape[0], kv_width), BF16),
                   jax.ShapeDtypeStruct((batch, kv_heads, n_chunks, hd, kv_chunk), BF16)],
        compiler_params=_params(("arbitrary",) * 3, blocks, 10 * kv_chunk * hd * 4),
        name="kv_norm_rope_transpose",
    )(proj, proj, cos, sin, _gain_rows(k_g), _pair_swap_matrix(hd))


def _attn_kernel(q_ref, k_ref, vt_ref, g_ref, cos_ref, sin_ref, qg_ref, swap_ref, o_ref,
                 qs_ref, s_ref, smax_ref, m_ref, l_ref, acc_ref, *,
                 groups, ctx_tiles, ctx_rows, n_chunks, kv_chunk, q_scale):
    t = pl.program_id(2)
    hd = ATTN_HEAD_DIM
    nt = (((1,), (1,)), ((), ()))

    qs_ref[...] = _norm_rope(q_ref[...], cos_ref[...], sin_ref[...], qg_ref[0:1, :], qg_ref[1:2, :],
                             swap_ref, q_scale).astype(qs_ref.dtype)

    m_ref[...] = jnp.full(m_ref.shape, -jnp.inf, F32)
    l_ref[...] = jnp.zeros(l_ref.shape, F32)
    acc_ref[...] = jnp.zeros(acc_ref.shape, F32)

    def scores(k, slot, n, h):
        s = lax.dot_general(k, qs_ref[:, h * hd:(h + 1) * hd], nt, preferred_element_type=F32)
        s_ref[slot, h, 0:n, :] = s
        smax_ref[slot, h] = jnp.max(s, axis=0, keepdims=True)

    def softmax_pv(vt, slot, n, h):
        s = s_ref[slot, h, 0:n, :]
        m_prev = m_ref[h]
        m_new = jnp.maximum(m_prev, smax_ref[slot, h])
        alpha = jnp.exp2(m_prev - m_new)
        p = jnp.exp2(s - m_new)
        l_ref[h] = alpha * l_ref[h] + jnp.sum(p, axis=0, keepdims=True)
        m_ref[h] = m_new
        acc_ref[h] = alpha * acc_ref[h] + jnp.dot(vt, p.astype(BF16), preferred_element_type=F32)

    def chunk_step(k_next, vt, slot, n):
        ahead = min(ATTN_SCORES_AHEAD, groups)
        if k_next is not None:
            for h in range(ahead):
                scores(k_next, 1 - slot, n, h)
        for h in range(groups):
            softmax_pv(vt, slot, n, h)
            if k_next is not None and h + ahead < groups:
                scores(k_next, 1 - slot, n, h + ahead)

    def k_chunk(c):
        return k_ref[pl.ds(pl.multiple_of(c * kv_chunk, kv_chunk), kv_chunk), :]

    @pl.when(t < ctx_tiles)
    def _():
        for r0 in range(0, ctx_rows, ROW_TILE):
            c, off = divmod(r0, kv_chunk)
            for h in range(groups):
                scores(k_ref[r0:r0 + ROW_TILE, :], 0, ROW_TILE, h)
            chunk_step(None, vt_ref[c, :, off:off + ROW_TILE], 0, ROW_TILE)

    @pl.when(t >= ctx_tiles)
    def _():
        for h in range(groups):
            scores(k_chunk(0), 0, kv_chunk, h)
        pairs = (n_chunks - 1) // 2

        def body(i, carry):
            c = 2 * i
            chunk_step(k_chunk(c + 1), vt_ref[c], 0, kv_chunk)
            chunk_step(k_chunk(c + 2), vt_ref[c + 1], 1, kv_chunk)
            return carry
        lax.fori_loop(0, pairs, body, 0)
        if n_chunks % 2 == 0:
            chunk_step(k_chunk(n_chunks - 1), vt_ref[n_chunks - 2], 0, kv_chunk)
            chunk_step(None, vt_ref[n_chunks - 1], 1, kv_chunk)
        else:
            chunk_step(None, vt_ref[n_chunks - 1], 0, kv_chunk)

    o = jnp.concatenate([(acc_ref[h] * (1.0 / l_ref[h])).T for h in range(groups)], axis=1)
    o_ref[...] = (o * _silu(g_ref[...].astype(F32))).astype(o_ref.dtype)


def _attention(proj, cos, sin, q_g, k_g, *, batch, tiles, ctx_tiles, d, kv_width):
    r = proj.shape[0]
    hd = ATTN_HEAD_DIM
    kv_heads = kv_width // hd
    groups = d // kv_width
    bw = groups * hd
    kv_rows = tiles * ROW_TILE
    kv_chunk = _largest_tile(kv_rows, (768, 512, 256))
    n_chunks = kv_rows // kv_chunk
    keys, vt = _kv_prep(proj, cos, sin, k_g, batch=batch, n_chunks=n_chunks, kv_chunk=kv_chunk,
                        d=d, kv_width=kv_width)
    kern = functools.partial(_attn_kernel, groups=groups, ctx_tiles=ctx_tiles, ctx_rows=ctx_tiles * ROW_TILE,
                             n_chunks=n_chunks, kv_chunk=kv_chunk, q_scale=hd ** -0.5 * math.log2(math.e))
    blocks = (3 * ROW_TILE * bw * 2 + 2 * kv_rows * hd * 2 + 2 * ROW_TILE * hd * 4 + MOD_ROWS * hd * 4
              + bw * bw * 2)
    temps = (2 * groups + 6) * kv_chunk * ROW_TILE * 4 + groups * (hd + 16) * ROW_TILE * 4 + 8 * ROW_TILE * bw * 4
    return pl.pallas_call(
        kern,
        grid=(batch, kv_heads, tiles),
        in_specs=[pl.BlockSpec((ROW_TILE, bw), lambda b, h, t: (b * tiles + t, h)),
                  pl.BlockSpec((kv_rows, hd), lambda b, h, t: (b, h)),
                  pl.BlockSpec((None, None, n_chunks, hd, kv_chunk), lambda b, h, t: (b, h, 0, 0, 0)),
                  pl.BlockSpec((ROW_TILE, bw), lambda b, h, t: (b * tiles + t, (d + 2 * kv_width) // bw + h)),
                  pl.BlockSpec((ROW_TILE, hd), lambda b, h, t: (t, 0)),
                  pl.BlockSpec((ROW_TILE, hd), lambda b, h, t: (t, 0)),
                  pl.BlockSpec((MOD_ROWS, hd), lambda b, h, t: (0, 0)),
                  pl.BlockSpec((bw, bw), lambda b, h, t: (0, 0))],
        out_specs=pl.BlockSpec((ROW_TILE, bw), lambda b, h, t: (b * tiles + t, h)),
        out_shape=jax.ShapeDtypeStruct((r, d), BF16),
        scratch_shapes=[pltpu.VMEM((ROW_TILE, bw), BF16),
                        pltpu.VMEM((2, groups, kv_chunk, ROW_TILE), F32),
                        pltpu.VMEM((2, groups, 1, ROW_TILE), F32),
                        pltpu.VMEM((groups, 1, ROW_TILE), F32),
                        pltpu.VMEM((groups, 1, ROW_TILE), F32),
                        pltpu.VMEM((groups, hd, ROW_TILE), F32)],
        compiler_params=_params(("arbitrary",) * 3, blocks, temps),
        name="attention",
    )(proj, keys, vt, proj, cos, sin, _gain_rows(q_g), _pair_swap_matrix(bw))


def _ret_chunk(q_ref, k_ref, v_ref, cos_ref, sin_ref, dec_ref, swap_ref, s_ref, *, qk_dim, hps, reverse):
    tc = ROW_TILE
    v_dim = 2 * qk_dim
    nt = (((1,), (1,)), ((), ()))
    cos, sin = cos_ref[...], sin_ref[...]
    ii = lax.broadcasted_iota(jnp.int32, (tc, tc), 0)
    jj = lax.broadcasted_iota(jnp.int32, (tc, tc), 1)
    pos = lax.broadcasted_iota(jnp.int32, (tc, 1), 0).astype(F32)
    if reverse:
        dist, keep = (jj - ii), jj > ii
        q_pow, k_pow = tc - pos, pos
    else:
        dist, keep = (ii - jj), ii >= jj
        q_pow, k_pow = pos + 1.0, tc - 1.0 - pos
    dist = jnp.maximum(dist, 0).astype(F32)

    raw = []
    for h in range(hps):
        qb = q_ref[:, h * qk_dim:(h + 1) * qk_dim]
        kb = k_ref[:, h * qk_dim:(h + 1) * qk_dim]
        raw.append((qb, kb, _swap_pairs(qb, swap_ref), _swap_pairs(kb, swap_ref)))

    heads = []
    for h, (qb, kb, q_partner, k_partner) in enumerate(raw):
        log_g = jnp.log1p(-jnp.exp(dec_ref[h]))[0:1, 0:1]
        q = qb.astype(F32) * cos + q_partner * sin
        k = (kb.astype(F32) * cos + k_partner * sin) * (qk_dim ** -0.5)
        scores = lax.dot_general(q.astype(BF16), k.astype(BF16), nt, preferred_element_type=F32)
        heads.append((log_g, q, k, scores))

    outs = []
    for h, (log_g, q, k, scores) in enumerate(heads):
        v = v_ref[:, h * v_dim:(h + 1) * v_dim]
        decay = jnp.where(keep, jnp.exp(dist * log_g), 0.0)
        lhs = jnp.concatenate([(scores * decay).astype(BF16),
                               (q * jnp.exp(q_pow * log_g)).astype(BF16)], axis=1)
        rhs = jnp.concatenate([v, s_ref[h].astype(BF16)], axis=0)
        outs.append(jnp.dot(lhs, rhs, preferred_element_type=F32))

    for h, (log_g, q, k, scores) in enumerate(heads):
        v = v_ref[:, h * v_dim:(h + 1) * v_dim]
        kt = (k * jnp.exp(k_pow * log_g)).T.astype(BF16)
        s_ref[h] = s_ref[h] * jnp.exp(tc * log_g) + jnp.dot(kt, v, preferred_element_type=F32)
    return outs


def _ret_fwd_kernel(q_ref, k_ref, v_ref, cos_ref, sin_ref, dec_ref, swap_ref, o_ref, s_ref, *, qk_dim, hps):
    @pl.when(pl.program_id(2) == 0)
    def _():
        s_ref[...] = jnp.zeros(s_ref.shape, F32)

    outs = _ret_chunk(q_ref, k_ref, v_ref, cos_ref, sin_ref, dec_ref, swap_ref, s_ref,
                      qk_dim=qk_dim, hps=hps, reverse=False)
    o_ref[...] = jnp.concatenate(outs, axis=1).astype(o_ref.dtype)


def _ret_bwd_kernel(q_ref, k_ref, v_ref, cos_ref, sin_ref, dec_ref, swap_ref, of_ref, g_ref, gn_ref, o_ref, s_ref,
                    *, qk_dim, hps):
    @pl.when(pl.program_id(2) == 0)
    def _():
        s_ref[...] = jnp.zeros(s_ref.shape, F32)

    outs = _ret_chunk(q_ref, k_ref, v_ref, cos_ref, sin_ref, dec_ref, swap_ref, s_ref,
                      qk_dim=qk_dim, hps=hps, reverse=True)
    v_dim = 2 * qk_dim
    normed = []
    for h, o in enumerate(outs):
        cols = slice(h * v_dim, (h + 1) * v_dim)
        o = o + of_ref[:, cols].astype(F32)
        mu = jnp.mean(o, axis=-1, keepdims=True)
        dev = o - mu
        var = jnp.mean(dev * dev, axis=-1, keepdims=True)
        normed.append(dev * lax.rsqrt(var + GN_EPS) * gn_ref[:, cols])
    o_ref[...] = (jnp.concatenate(normed, axis=1) * _silu(g_ref[...].astype(F32))).astype(o_ref.dtype)


def _retention(proj, cos, sin, decay_f, decay_b, gn_g, *, batch, tiles, ctx_tiles, d, heads):
    r = proj.shape[0]
    qk_dim = d // heads
    v_dim = 2 * qk_dim
    tc = ROW_TILE
    dec_f = jnp.broadcast_to(decay_f.astype(F32)[:, None, None], (heads, 8, 128))
    dec_b = jnp.broadcast_to(decay_b.astype(F32)[:, None, None], (heads, 8, 128))
    hps = _largest_tile(heads, (RET_HEADS_PER_STEP, 1))
    swap = _pair_swap_matrix(qk_dim)
    qw, vw = hps * qk_dim, hps * v_dim
    k_col0 = d // qw
    v_col0 = 2 * d // vw
    g_col0 = (2 * d + heads * v_dim) // vw
    blocks = 2 * tc * qw * 2 + 4 * tc * vw * 2 + 2 * tc * qk_dim * 4 + hps * 8 * 128 * 4 + vw * 4
    temps = hps * (qk_dim * v_dim * 4 + 12 * tc * v_dim * 4)

    def fwd_row(b, h, c):
        return b * tiles + c

    def bwd_chunk(c):
        return jnp.where(c < ctx_tiles, ctx_tiles - 1 - c, tiles - 1 - (c - ctx_tiles))

    def bwd_row(b, h, c):
        return b * tiles + bwd_chunk(c)

    o_f = pl.pallas_call(
        functools.partial(_ret_fwd_kernel, qk_dim=qk_dim, hps=hps),
        grid=(batch, heads // hps, tiles),
        in_specs=[pl.BlockSpec((tc, qw), lambda b, h, c: (fwd_row(b, h, c), h)),
                  pl.BlockSpec((tc, qw), lambda b, h, c: (fwd_row(b, h, c), k_col0 + h)),
                  pl.BlockSpec((tc, vw), lambda b, h, c: (fwd_row(b, h, c), v_col0 + h)),
                  pl.BlockSpec((tc, qk_dim), lambda b, h, c: (c, 0)),
                  pl.BlockSpec((tc, qk_dim), lambda b, h, c: (c, 0)),
                  pl.BlockSpec((hps, 8, 128), lambda b, h, c: (h, 0, 0)),
                  pl.BlockSpec((qk_dim, qk_dim), lambda b, h, c: (0, 0))],
        out_specs=pl.BlockSpec((tc, vw), lambda b, h, c: (fwd_row(b, h, c), h)),
        out_shape=jax.ShapeDtypeStruct((r, heads * v_dim), BF16),
        scratch_shapes=[pltpu.VMEM((hps, qk_dim, v_dim), F32)],
        compiler_params=_params(("arbitrary",) * 3, blocks, temps),
        name="retention_fwd",
    )(proj, proj, proj, cos, sin, dec_f, swap)

    return pl.pallas_call(
        functools.partial(_ret_bwd_kernel, qk_dim=qk_dim, hps=hps),
        grid=(batch, heads // hps, tiles),
        in_specs=[pl.BlockSpec((tc, qw), lambda b, h, c: (bwd_row(b, h, c), h)),
                  pl.BlockSpec((tc, qw), lambda b, h, c: (bwd_row(b, h, c), k_col0 + h)),
                  pl.BlockSpec((tc, vw), lambda b, h, c: (bwd_row(b, h, c), v_col0 + h)),
                  pl.BlockSpec((tc, qk_dim), lambda b, h, c: (bwd_chunk(c), 0)),
                  pl.BlockSpec((tc, qk_dim), lambda b, h, c: (bwd_chunk(c), 0)),
                  pl.BlockSpec((hps, 8, 128), lambda b, h, c: (h, 0, 0)),
                  pl.BlockSpec((qk_dim, qk_dim), lambda b, h, c: (0, 0)),
                  pl.BlockSpec((tc, vw), lambda b, h, c: (bwd_row(b, h, c), h)),
                  pl.BlockSpec((tc, vw), lambda b, h, c: (bwd_row(b, h, c), g_col0 + h)),
                  pl.BlockSpec((1, vw), lambda b, h, c: (0, h))],
        out_specs=pl.BlockSpec((tc, vw), lambda b, h, c: (bwd_row(b, h, c), h)),
        out_shape=jax.ShapeDtypeStruct((r, heads * v_dim), BF16),
        scratch_shapes=[pltpu.VMEM((hps, qk_dim, v_dim), F32)],
        compiler_params=_params(("arbitrary",) * 3, blocks, temps),
        name="retention_bwd_norm_gate",
    )(proj, proj, proj, cos, sin, dec_b, swap, o_f, proj, gn_g.astype(F32).reshape(1, heads * v_dim))


def _final_norm_kernel(x_ref, g_ref, o_ref):
    x = x_ref[...]
    o_ref[...] = x * lax.rsqrt(jnp.mean(x * x, axis=-1, keepdims=True) + NORM_EPS) * g_ref[...]


def _final_norm(xs, g, *, batch, tiles, ctx_tiles):
    d = xs.shape[1]
    seq_tiles = tiles - ctx_tiles
    blocks = 2 * ROW_TILE * d * 4 + d * 4
    return pl.pallas_call(
        _final_norm_kernel,
        grid=(batch, seq_tiles),
        in_specs=[pl.BlockSpec((ROW_TILE, d), lambda b, t: (b * tiles + ctx_tiles + t, 0)),
                  pl.BlockSpec((1, d), lambda b, t: (0, 0))],
        out_specs=pl.BlockSpec((ROW_TILE, d), lambda b, t: (b * seq_tiles + t, 0)),
        out_shape=jax.ShapeDtypeStruct((batch * seq_tiles * ROW_TILE, d), F32),
        compiler_params=_params(("arbitrary", "arbitrary"), blocks, 2 * ROW_TILE * d * 4),
        name="final_norm",
    )(xs, g.reshape(1, d))


def kernel(x, c, ctx, c_ctx, norm_g, mod_w, mod_b, pool_w_in, pool_w_grp, pool_scale, pool_w_out,
           attn_w_in, attn_q_norm, attn_k_norm, attn_w_out, ret_w_in, ret_decay_fwd, ret_decay_bwd,
           ret_gn_g, ret_w_out, final_norm_g):
    batch, seq, d = x.shape
    n_ctx = ctx.shape[1]
    depth = norm_g.shape[0]
    assert seq % ROW_TILE == 0 and n_ctx % ROW_TILE == 0 and seq % GRID_W == 0
    assert batch < MOD_ROWS
    ctx_tiles = n_ctx // ROW_TILE
    tiles = (n_ctx + seq) // ROW_TILE
    rows_per_sample = n_ctx + seq
    tm = _largest_tile(rows_per_sample, (768, 512, 256))

    xs = jnp.concatenate([ctx, x], axis=1).reshape(batch * rows_per_sample, d)
    cvec = jnp.concatenate([c, c_ctx[None, :], jnp.zeros((MOD_ROWS - batch - 1, d), F32)], axis=0)
    mods = _mod_tables(cvec, mod_w, mod_b)
    geom = dict(batch=batch, tiles=tiles, ctx_tiles=ctx_tiles)

    kv_width = (attn_w_in.shape[2] - 2 * d) // 2
    ret_heads = ret_decay_fwd.shape[1]
    attn_cos, attn_sin = _rope_tables(n_ctx, seq, ATTN_HEAD_DIM)
    ret_cos, ret_sin = _rope_tables(n_ctx, seq, d // ret_heads)

    pool_w_out_b, attn_w_out_b, ret_w_out_b = (w.astype(BF16) for w in (pool_w_out, attn_w_out, ret_w_out))
    for i in range(depth):
        kind, j = i % 3, i // 3
        h = _norm_mod(xs, norm_g[i], mods[i], **geom)
        if kind == 0:
            width = pool_w_in.shape[2] // 2
            u = _matmul(h, _fold_group_mix(pool_w_in, pool_w_grp, pool_scale, j)[None], 0, name="in_proj_mixed")
            g = _matmul(h, pool_w_in, j, col0=width, n_cols=width)
            z = _pool_mix(u, g, n_groups=pool_w_grp.shape[1], **geom)
            w_out = pool_w_out_b
        elif kind == 1:
            proj = _matmul(h, attn_w_in, j)
            z = _attention(proj, attn_cos, attn_sin, attn_q_norm[j], attn_k_norm[j],
                           **geom, d=d, kv_width=kv_width)
            w_out = attn_w_out_b
        else:
            proj = _matmul(h, ret_w_in, j)
            z = _retention(proj, ret_cos, ret_sin, ret_decay_fwd[j], ret_decay_bwd[j], ret_gn_g[j],
                           **geom, d=d, heads=ret_heads)
            w_out = ret_w_out_b
        xs = _matmul_residual(z, w_out, j, xs, mods[i], batch=batch, tm=tm, ctx_rows=n_ctx)

    out = _final_norm(xs, final_norm_g, **geom)
    return out.reshape(batch, seq, d)
```

```python
import functools
import math

import jax
import jax.numpy as jnp
from jax import lax
from jax.experimental import pallas as pl
from jax.experimental.pallas import tpu as pltpu

F32 = jnp.float32
BF16 = jnp.bfloat16

GRID_W = 64
ATTN_HEAD_DIM = 128
POOL_WINDOWS = (2, 4, 8, 16)
ROPE_THETA = 10000.0
NORM_EPS = 1e-6
GN_EPS = 1e-5

RET_HEADS_PER_STEP = 4
ATTN_SCORES_AHEAD = 3
ROW_TILE = 256
HALO_ROWS = 16
EDGE = max(POOL_WINDOWS) // 2
MOD_ROWS = 8
V7X_VMEM_BYTES = 64 * 1024 * 1024
VMEM_CAP_BYTES = V7X_VMEM_BYTES - 8 * 1024 * 1024


def _vmem_limit(block_bytes, temp_bytes=0):
    need = 2 * block_bytes + temp_bytes + 4 * 1024 * 1024
    return int(min(max(need, 16 * 1024 * 1024), VMEM_CAP_BYTES))


def _params(semantics, block_bytes, temp_bytes=0):
    return pltpu.CompilerParams(dimension_semantics=semantics,
                                vmem_limit_bytes=_vmem_limit(block_bytes, temp_bytes))


def _largest_tile(n, candidates):
    for c in candidates:
        if n % c == 0:
            return c
    raise ValueError(f"no tile among {candidates} divides {n}")


def _silu(v):
    return v * (1.0 / (1.0 + jnp.exp(-v)))


def _mod_kernel(c_ref, w_ref, b_ref, o_ref):
    a = _silu(c_ref[...]).astype(BF16)
    w = w_ref[0].astype(BF16)
    o_ref[0] = jnp.dot(a, w, preferred_element_type=F32) + b_ref[0]


def _mod_tables(cvec, mod_w, mod_b):
    depth, d, n = mod_w.shape
    tn = _largest_tile(n, (512, 256, 128))
    blocks = MOD_ROWS * d * 4 + d * tn * 4 + tn * 4 + MOD_ROWS * tn * 4
    return pl.pallas_call(
        _mod_kernel,
        grid=(depth, n // tn),
        in_specs=[pl.BlockSpec((MOD_ROWS, d), lambda l, j: (0, 0)),
                  pl.BlockSpec((1, d, tn), lambda l, j: (l, 0, j)),
                  pl.BlockSpec((1, 1, tn), lambda l, j: (l, 0, j))],
        out_specs=pl.BlockSpec((1, MOD_ROWS, tn), lambda l, j: (l, 0, j)),
        out_shape=jax.ShapeDtypeStruct((depth, MOD_ROWS, n), F32),
        compiler_params=_params(("arbitrary", "arbitrary"), blocks, d * tn * 2),
        name="adaln_tables",
    )(cvec, mod_w, mod_b.reshape(depth, 1, n))


def _norm_mod_kernel(x_ref, g_ref, mod_ref, o_ref, *, d, ctx_tiles, ctx_row):
    b, t = pl.program_id(0), pl.program_id(1)
    row = jnp.where(t < ctx_tiles, ctx_row, b)
    shift = mod_ref[pl.ds(row, 1), 0:d]
    scale = mod_ref[pl.ds(row, 1), d:2 * d]
    x = x_ref[...]
    y = x * lax.rsqrt(jnp.mean(x * x, axis=-1, keepdims=True) + NORM_EPS) * g_ref[...]
    o_ref[...] = (y * (1.0 + scale) + shift).astype(o_ref.dtype)


def _join_norm_mod_kernel(ctx_ref, x_ref, g_ref, mod_ref, o_ref, xs_ref, *, d, ctx_tiles, ctx_row):
    b, t = pl.program_id(0), pl.program_id(1)

    def emit(x, row):
        shift = mod_ref[pl.ds(row, 1), 0:d]
        scale = mod_ref[pl.ds(row, 1), d:2 * d]
        y = x * lax.rsqrt(jnp.mean(x * x, axis=-1, keepdims=True) + NORM_EPS) * g_ref[...]
        o_ref[...] = (y * (1.0 + scale) + shift).astype(o_ref.dtype)
        xs_ref[...] = x

    @pl.when(t < ctx_tiles)
    def _():
        emit(ctx_ref[...], ctx_row)

    @pl.when(t >= ctx_tiles)
    def _():
        emit(x_ref[...], b)


def _join_norm_mod(ctx, x, g, mods, *, batch, tiles, ctx_tiles):
    d = x.shape[-1]
    seq_tiles = tiles - ctx_tiles
    r = batch * tiles * ROW_TILE
    blocks = 3 * ROW_TILE * d * 4 + d * 4 + MOD_ROWS * 3 * d * 4 + ROW_TILE * d * 2
    return pl.pallas_call(
        functools.partial(_join_norm_mod_kernel, d=d, ctx_tiles=ctx_tiles, ctx_row=batch),
        grid=(batch, tiles),
        in_specs=[pl.BlockSpec((ROW_TILE, d), lambda b, t: (b * ctx_tiles + jnp.minimum(t, ctx_tiles - 1), 0)),
                  pl.BlockSpec((ROW_TILE, d), lambda b, t: (b * seq_tiles + jnp.maximum(t - ctx_tiles, 0), 0)),
                  pl.BlockSpec((1, d), lambda b, t: (0, 0)),
                  pl.BlockSpec((MOD_ROWS, 3 * d), lambda b, t: (0, 0))],
        out_specs=[pl.BlockSpec((ROW_TILE, d), lambda b, t: (b * tiles + t, 0)),
                   pl.BlockSpec((ROW_TILE, d), lambda b, t: (b * tiles + t, 0))],
        out_shape=[jax.ShapeDtypeStruct((r, d), BF16), jax.ShapeDtypeStruct((r, d), F32)],
        compiler_params=_params(("arbitrary", "arbitrary"), blocks, 3 * ROW_TILE * d * 4),
        name="join_norm_modulate",
    )(ctx.reshape(-1, d), x.reshape(-1, d), g.reshape(1, d), mods)


def _norm_mod(xs, g, mods, *, batch, tiles, ctx_tiles):
    r, d = xs.shape
    blocks = ROW_TILE * d * 4 + d * 4 + MOD_ROWS * 3 * d * 4 + ROW_TILE * d * 2
    return pl.pallas_call(
        functools.partial(_norm_mod_kernel, d=d, ctx_tiles=ctx_tiles, ctx_row=batch),
        grid=(batch, tiles),
        in_specs=[pl.BlockSpec((ROW_TILE, d), lambda b, t: (b * tiles + t, 0)),
                  pl.BlockSpec((1, d), lambda b, t: (0, 0)),
                  pl.BlockSpec((MOD_ROWS, 3 * d), lambda b, t: (0, 0))],
        out_specs=pl.BlockSpec((ROW_TILE, d), lambda b, t: (b * tiles + t, 0)),
        out_shape=jax.ShapeDtypeStruct((r, d), BF16),
        compiler_params=_params(("arbitrary", "arbitrary"), blocks, 3 * ROW_TILE * d * 4),
        name="norm_modulate",
    )(xs, g.reshape(1, d), mods)


def _matmul_kernel(a_ref, b_ref, o_ref):
    o_ref[...] = jnp.dot(a_ref[...], b_ref[...].astype(BF16), preferred_element_type=F32).astype(o_ref.dtype)


def _matmul(a, w, layer, *, col0=0, n_cols=None, name="in_proj"):
    m, k = a.shape
    n_cols = w.shape[2] - col0 if n_cols is None else n_cols
    wbytes = jnp.dtype(w.dtype).itemsize
    tn = _largest_tile(math.gcd(n_cols, col0) if col0 else n_cols, (1024, 512, 256, 128) if wbytes == 2 else (512, 256, 128))
    for tm in (1536, 1024, 768, 512, 256):
        blocks = tm * k * 2 + k * tn * wbytes + tm * tn * 2
        temps = tm * tn * (4 + 2) + (k * tn * 2 if wbytes == 4 else 0)
        if m % tm == 0 and 2 * blocks + temps + 4 * 1024 * 1024 <= VMEM_CAP_BYTES:
            break
    else:
        raise ValueError("in-projection blocks do not fit VMEM")
    return pl.pallas_call(
        _matmul_kernel,
        grid=(m // tm, n_cols // tn),
        in_specs=[pl.BlockSpec((tm, k), lambda i, j: (i, 0)),
                  pl.BlockSpec((None, k, tn), lambda i, j: (layer, 0, col0 // tn + j))],
        out_specs=pl.BlockSpec((tm, tn), lambda i, j: (i, j)),
        out_shape=jax.ShapeDtypeStruct((m, n_cols), BF16),
        compiler_params=_params(("arbitrary", "arbitrary"), blocks, temps),
        name=name,
    )(a, w)


def _matmul_res_kernel(a_ref, b_ref, x_ref, gate_ref, o_ref, *, tm, ctx_rows, ctx_row):
    b, t = pl.program_id(0), pl.program_id(1)
    y = jnp.dot(a_ref[...], b_ref[...].astype(BF16), preferred_element_type=F32)
    gate_b = gate_ref[pl.ds(b, 1), :]
    gate_c = gate_ref[pl.ds(ctx_row, 1), :]
    rows = t * tm + lax.broadcasted_iota(jnp.int32, (tm, 1), 0)
    gate = jnp.where(rows < ctx_rows, gate_c, gate_b)
    o_ref[...] = x_ref[...] + gate * y


def _matmul_residual(z, w, layer, xs, mods, *, batch, tm, ctx_rows):
    m, kdim = z.shape
    d = w.shape[2]
    wbytes = jnp.dtype(w.dtype).itemsize
    for tn in (1024, 512, 256, 128):
        blocks = tm * kdim * 2 + kdim * tn * wbytes + 2 * tm * tn * 4 + MOD_ROWS * tn * 4
        temps = 2 * tm * tn * 4 + (kdim * tn * 2 if wbytes == 4 else 0)
        if d % tn == 0 and 2 * blocks + temps + 4 * 1024 * 1024 <= VMEM_CAP_BYTES:
            break
    else:
        raise ValueError("out-projection blocks do not fit VMEM")
    tiles = m // (batch * tm)
    gate_col0 = 2 * d // tn
    return pl.pallas_call(
        functools.partial(_matmul_res_kernel, tm=tm, ctx_rows=ctx_rows, ctx_row=batch),
        grid=(batch, tiles, d // tn),
        in_specs=[pl.BlockSpec((tm, kdim), lambda b, t, j: (b * tiles + t, 0)),
                  pl.BlockSpec((None, kdim, tn), lambda b, t, j: (layer, 0, j)),
                  pl.BlockSpec((tm, tn), lambda b, t, j: (b * tiles + t, j)),
                  pl.BlockSpec((MOD_ROWS, tn), lambda b, t, j: (0, gate_col0 + j))],
        out_specs=pl.BlockSpec((tm, tn), lambda b, t, j: (b * tiles + t, j)),
        out_shape=jax.ShapeDtypeStruct((m, d), F32),
        input_output_aliases={2: 0},
        compiler_params=_params(("arbitrary",) * 3, blocks, temps),
        name="out_proj_residual",
    )(z, w, xs, mods)


def _edge_window_sums(x, window):
    half = window // 2
    total = x[EDGE - half:2 * EDGE - half, :]
    for off in range(-half + 1, window - half):
        total = total + x[EDGE + off:2 * EDGE + off, :]
    return total


def _pool_kernel(up_ref, u_ref, un_ref, g_ref, o_ref, *,
                 ctx_tiles, tiles, ctx_rows, seq_rows):
    t = pl.program_id(1)
    tr = ROW_TILE
    gdim = u_ref.shape[1] // len(POOL_WINDOWS)
    in_ctx = t < ctx_tiles
    first = (t == 0) | (t == ctx_tiles)
    last = (t == ctx_tiles - 1) | (t == tiles - 1)
    seq_tile = jnp.where(in_ctx, t, t - ctx_tiles)
    n_seq = jnp.where(in_ctx, ctx_rows, seq_rows)
    pos = seq_tile * tr + lax.broadcasted_iota(jnp.int32, (tr, 1), 0)
    delta = lax.broadcasted_iota(jnp.int32, (tr, tr), 1) - lax.broadcasted_iota(jnp.int32, (tr, tr), 0)

    for gidx, window in enumerate(POOL_WINDOWS):
        cols = slice(gidx * gdim, (gidx + 1) * gdim)
        half = window // 2
        ub = u_ref[:, cols]
        above = jnp.where(first, 0.0, up_ref[HALO_ROWS - EDGE:HALO_ROWS, cols].astype(F32))
        below = jnp.where(last, 0.0, un_ref[0:EDGE, cols].astype(F32))
        u_head = ub[0:2 * EDGE, :].astype(F32)
        u_tail = ub[tr - 2 * EDGE:tr, :].astype(F32)
        band = (jnp.where((delta >= -half) & (delta < window - half), 1.0 / window, 0.0)
                - jnp.where(delta == 0, 1.0, 0.0)).astype(BF16)
        inner = jnp.dot(band, ub, preferred_element_type=F32)
        cnt = jnp.minimum(pos + (window - half), n_seq) - jnp.maximum(pos - half, 0)
        inv = 1.0 / cnt.astype(F32)
        head = (_edge_window_sums(jnp.concatenate([above, u_head], axis=0), window) * inv[0:EDGE, :]
                - u_head[0:EDGE, :])
        tail = (_edge_window_sums(jnp.concatenate([u_tail, below], axis=0), window) * inv[tr - EDGE:tr, :]
                - u_tail[EDGE:2 * EDGE, :])
        mixed = jnp.concatenate([head, inner[EDGE:tr - EDGE, :], tail], axis=0)
        o_ref[:, cols] = (mixed * _silu(g_ref[:, cols].astype(F32))).astype(o_ref.dtype)


def _fold_kernel(a_ref, b_ref, scale_ref, o_ref, bw_ref):
    @pl.when(pl.program_id(1) == 0)
    def _():
        bw_ref[...] = b_ref[...].astype(BF16)

    mixed = jnp.dot(a_ref[...].astype(BF16), bw_ref[...], preferred_element_type=F32)
    o_ref[...] = (mixed * scale_ref[...]).astype(o_ref.dtype)


def _fold_group_mix(w_in, w_grp, scale, layer):
    d = w_in.shape[1]
    n_groups, gdim = w_grp.shape[1], w_grp.shape[2]
    tm = _largest_tile(d, (512, 256, 128))
    blocks = tm * gdim * 4 + gdim * gdim * 4 + tm * gdim * 2
    return pl.pallas_call(
        _fold_kernel,
        grid=(n_groups, d // tm),
        in_specs=[pl.BlockSpec((None, tm, gdim), lambda g, i: (layer, i, g)),
                  pl.BlockSpec((None, None, gdim, gdim), lambda g, i: (layer, g, 0, 0)),
                  pl.BlockSpec((None, 1, gdim), lambda g, i: (layer, 0, g))],
        out_specs=pl.BlockSpec((tm, gdim), lambda g, i: (i, g)),
        out_shape=jax.ShapeDtypeStruct((d, n_groups * gdim), BF16),
        scratch_shapes=[pltpu.VMEM((gdim, gdim), BF16)],
        compiler_params=_params(("arbitrary", "arbitrary"), blocks, gdim * gdim * 2 + 2 * tm * gdim * 4),
        name="fold_group_mix",
    )(w_in, w_grp, scale.reshape(scale.shape[0], 1, scale.shape[1]))


def _pool_mix(u, g, *, n_groups, batch, tiles, ctx_tiles):
    r, width = u.shape
    gdim = width // n_groups
    tr = ROW_TILE
    hb = tr // HALO_ROWS
    n_halo_blocks = r // HALO_ROWS
    kern = functools.partial(_pool_kernel, ctx_tiles=ctx_tiles, tiles=tiles,
                             ctx_rows=ctx_tiles * tr, seq_rows=(tiles - ctx_tiles) * tr)
    assert n_groups == len(POOL_WINDOWS)
    blocks = 3 * tr * width * 2 + 2 * HALO_ROWS * width * 2
    return pl.pallas_call(
        kern,
        grid=(batch, tiles),
        in_specs=[pl.BlockSpec((HALO_ROWS, width), lambda b, t: (jnp.maximum((b * tiles + t) * hb - 1, 0), 0)),
                  pl.BlockSpec((tr, width), lambda b, t: (b * tiles + t, 0)),
                  pl.BlockSpec((HALO_ROWS, width),
                               lambda b, t: (jnp.minimum((b * tiles + t + 1) * hb, n_halo_blocks - 1), 0)),
                  pl.BlockSpec((tr, width), lambda b, t: (b * tiles + t, 0))],
        out_specs=pl.BlockSpec((tr, width), lambda b, t: (b * tiles + t, 0)),
        out_shape=jax.ShapeDtypeStruct((r, width), BF16),
        compiler_params=_params(("arbitrary",) * 2, blocks, 10 * tr * gdim * 4),
        name="pool_mix",
    )(u, u, u, g)


def _rope_tables(n_ctx, n_seq, head_dim):
    rows = n_seq // GRID_W
    row = jnp.repeat(jnp.arange(rows, dtype=F32), GRID_W)
    col = jnp.tile(jnp.arange(GRID_W, dtype=F32), rows)
    n_freq = head_dim // 4
    inv = ROPE_THETA ** (-jnp.arange(n_freq, dtype=F32) / n_freq)
    ang = jnp.concatenate([row[:, None] * inv, col[:, None] * inv], axis=-1)
    ang = jnp.concatenate([jnp.zeros((n_ctx, head_dim // 2), F32), ang], axis=0)
    cos = jnp.repeat(jnp.cos(ang), 2, axis=-1)
    sin = jnp.repeat(jnp.sin(ang), 2, axis=-1)
    sign = jnp.tile(jnp.array([-1.0, 1.0], F32), head_dim // 2)
    return cos, sin * sign


def _pair_swap_matrix(n):
    idx = jnp.arange(n)
    return (idx[:, None] == (idx[None, :] ^ 1)).astype(BF16)


def _swap_pairs(x_bf16, swap_ref):
    return jnp.dot(x_bf16, swap_ref[...], preferred_element_type=F32)


def _norm_rope(xb, cos, sin, gain, gain_swapped, swap_ref, out_scale):
    hd = ATTN_HEAD_DIM
    x = xb.astype(F32)
    partner = _swap_pairs(xb, swap_ref)
    outs = []
    for h in range(xb.shape[1] // hd):
        cols = slice(h * hd, (h + 1) * hd)
        xh = x[:, cols]
        inv = lax.rsqrt(jnp.mean(xh * xh, axis=-1, keepdims=True) + NORM_EPS) * out_scale
        outs.append(xh * inv * gain * cos + partner[:, cols] * inv * gain_swapped * sin)
    return outs[0] if len(outs) == 1 else jnp.concatenate(outs, axis=1)


def _gain_rows(g):
    hd = ATTN_HEAD_DIM
    swapped = g.reshape(hd // 2, 2)[:, ::-1].reshape(hd)
    return jnp.concatenate([g[None], swapped[None], jnp.zeros((MOD_ROWS - 2, hd), g.dtype)], axis=0).astype(F32)


def _kv_prep_kernel(k_ref, v_ref, cos_ref, sin_ref, g_ref, swap_ref, ko_ref, vt_ref):
    k = _norm_rope(k_ref[...], cos_ref[...], sin_ref[...], g_ref[0:1, :], g_ref[1:2, :], swap_ref, 1.0)
    ko_ref[...] = k.astype(ko_ref.dtype)
    vt_ref[...] = v_ref[...].astype(F32).T.astype(vt_ref.dtype)


def _kv_prep(proj, cos, sin, k_g, *, batch, n_chunks, kv_chunk, d, kv_width):
    hd = ATTN_HEAD_DIM
    kv_heads = kv_width // hd
    k_col0 = d // hd
    v_col0 = (d + kv_width) // hd
    blocks = 4 * kv_chunk * hd * 2 + 2 * kv_chunk * hd * 4 + MOD_ROWS * hd * 4 + hd * hd * 2
    return pl.pallas_call(
        _kv_prep_kernel,
        grid=(batch, kv_heads, n_chunks),
        in_specs=[pl.BlockSpec((kv_chunk, hd), lambda b, h, c: (b * n_chunks + c, k_col0 + h)),
                  pl.BlockSpec((kv_chunk, hd), lambda b, h, c: (b * n_chunks + c, v_col0 + h)),
                  pl.BlockSpec((kv_chunk, hd), lambda b, h, c: (c, 0)),
                  pl.BlockSpec((kv_chunk, hd), lambda b, h, c: (c, 0)),
                  pl.BlockSpec((MOD_ROWS, hd), lambda b, h, c: (0, 0)),
                  pl.BlockSpec((hd, hd), lambda b, h, c: (0, 0))],
        out_specs=[pl.BlockSpec((kv_chunk, hd), lambda b, h, c: (b * n_chunks + c, h)),
                   pl.BlockSpec((None, None, None, hd, kv_chunk), lambda b, h, c: (b, h, c, 0, 0))],
        out_shape=[jax.ShapeDtypeStruct((proj.shape[0], kv_width), BF16),
                   jax.ShapeDtypeStruct((batch, kv_heads, n_chunks, hd, kv_chunk), BF16)],
        compiler_params=_params(("arbitrary",) * 3, blocks, 10 * kv_chunk * hd * 4),
        name="kv_norm_rope_transpose",
    )(proj, proj, cos, sin, _gain_rows(k_g), _pair_swap_matrix(hd))


def _attn_kernel(q_ref, k_ref, vt_ref, g_ref, cos_ref, sin_ref, qg_ref, swap_ref, o_ref,
                 qs_ref, s_ref, smax_ref, m_ref, l_ref, acc_ref, *,
                 groups, ctx_tiles, ctx_rows, n_chunks, kv_chunk, q_scale):
    t = pl.program_id(2)
    hd = ATTN_HEAD_DIM
    nt = (((1,), (1,)), ((), ()))

    qs_ref[...] = _norm_rope(q_ref[...], cos_ref[...], sin_ref[...], qg_ref[0:1, :], qg_ref[1:2, :],
                             swap_ref, q_scale).astype(qs_ref.dtype)

    m_ref[...] = jnp.full(m_ref.shape, -jnp.inf, F32)
    l_ref[...] = jnp.zeros(l_ref.shape, F32)
    acc_ref[...] = jnp.zeros(acc_ref.shape, F32)

    def scores(k, slot, n, h):
        s = lax.dot_general(k, qs_ref[:, h * hd:(h + 1) * hd], nt, preferred_element_type=F32)
        s_ref[slot, h, 0:n, :] = s
        smax_ref[slot, h] = jnp.max(s, axis=0, keepdims=True)

    def softmax_pv(vt, slot, n, h):
        s = s_ref[slot, h, 0:n, :]
        m_prev = m_ref[h]
        m_new = jnp.maximum(m_prev, smax_ref[slot, h])
        alpha = jnp.exp2(m_prev - m_new)
        p = jnp.exp2(s - m_new)
        l_ref[h] = alpha * l_ref[h] + jnp.sum(p, axis=0, keepdims=True)
        m_ref[h] = m_new
        acc_ref[h] = alpha * acc_ref[h] + jnp.dot(vt, p.astype(BF16), preferred_element_type=F32)

    ahead = min(ATTN_SCORES_AHEAD, groups)

    def chunk_step(k_cur, k_next, vt, slot, n):
        for h in range(groups):
            if h + ahead < groups:
                scores(k_cur, slot, n, h + ahead)
            elif k_next is not None:
                scores(k_next, 1 - slot, n, h + ahead - groups)
            softmax_pv(vt, slot, n, h)

    def k_chunk(c):
        return k_ref[pl.ds(pl.multiple_of(c * kv_chunk, kv_chunk), kv_chunk), :]

    @pl.when(t < ctx_tiles)
    def _():
        for r0 in range(0, ctx_rows, ROW_TILE):
            c, off = divmod(r0, kv_chunk)
            keys = k_ref[r0:r0 + ROW_TILE, :]
            for h in range(ahead):
                scores(keys, 0, ROW_TILE, h)
            chunk_step(keys, None, vt_ref[c, :, off:off + ROW_TILE], 0, ROW_TILE)

    @pl.when(t >= ctx_tiles)
    def _():
        for h in range(ahead):
            scores(k_chunk(0), 0, kv_chunk, h)
        pairs = (n_chunks - 1) // 2

        def body(i, carry):
            c = 2 * i
            chunk_step(k_chunk(c), k_chunk(c + 1), vt_ref[c], 0, kv_chunk)
            chunk_step(k_chunk(c + 1), k_chunk(c + 2), vt_ref[c + 1], 1, kv_chunk)
            return carry
        lax.fori_loop(0, pairs, body, 0)
        last = n_chunks - 1
        if n_chunks % 2 == 0:
            chunk_step(k_chunk(last - 1), k_chunk(last), vt_ref[last - 1], 0, kv_chunk)
            chunk_step(k_chunk(last), None, vt_ref[last], 1, kv_chunk)
        else:
            chunk_step(k_chunk(last), None, vt_ref[last], 0, kv_chunk)

    o = jnp.concatenate([(acc_ref[h] * (1.0 / l_ref[h])).T for h in range(groups)], axis=1)
    o_ref[...] = (o * _silu(g_ref[...].astype(F32))).astype(o_ref.dtype)


def _attention(proj, cos, sin, q_g, k_g, *, batch, tiles, ctx_tiles, d, kv_width):
    r = proj.shape[0]
    hd = ATTN_HEAD_DIM
    kv_heads = kv_width // hd
    groups = d // kv_width
    bw = groups * hd
    kv_rows = tiles * ROW_TILE
    kv_chunk = _largest_tile(kv_rows, (768, 512, 256))
    n_chunks = kv_rows // kv_chunk
    keys, vt = _kv_prep(proj, cos, sin, k_g, batch=batch, n_chunks=n_chunks, kv_chunk=kv_chunk,
                        d=d, kv_width=kv_width)
    kern = functools.partial(_attn_kernel, groups=groups, ctx_tiles=ctx_tiles, ctx_rows=ctx_tiles * ROW_TILE,
                             n_chunks=n_chunks, kv_chunk=kv_chunk, q_scale=hd ** -0.5 * math.log2(math.e))
    blocks = (3 * ROW_TILE * bw * 2 + 2 * kv_rows * hd * 2 + 2 * ROW_TILE * hd * 4 + MOD_ROWS * hd * 4
              + bw * bw * 2)
    temps = (2 * groups + 6) * kv_chunk * ROW_TILE * 4 + groups * (hd + 16) * ROW_TILE * 4 + 8 * ROW_TILE * bw * 4
    return pl.pallas_call(
        kern,
        grid=(batch, kv_heads, tiles),
        in_specs=[pl.BlockSpec((ROW_TILE, bw), lambda b, h, t: (b * tiles + t, h)),
                  pl.BlockSpec((kv_rows, hd), lambda b, h, t: (b, h)),
                  pl.BlockSpec((None, None, n_chunks, hd, kv_chunk), lambda b, h, t: (b, h, 0, 0, 0)),
                  pl.BlockSpec((ROW_TILE, bw), lambda b, h, t: (b * tiles + t, (d + 2 * kv_width) // bw + h)),
                  pl.BlockSpec((ROW_TILE, hd), lambda b, h, t: (t, 0)),
                  pl.BlockSpec((ROW_TILE, hd), lambda b, h, t: (t, 0)),
                  pl.BlockSpec((MOD_ROWS, hd), lambda b, h, t: (0, 0)),
                  pl.BlockSpec((bw, bw), lambda b, h, t: (0, 0))],
        out_specs=pl.BlockSpec((ROW_TILE, bw), lambda b, h, t: (b * tiles + t, h)),
        out_shape=jax.ShapeDtypeStruct((r, d), BF16),
        scratch_shapes=[pltpu.VMEM((ROW_TILE, bw), BF16),
                        pltpu.VMEM((2, groups, kv_chunk, ROW_TILE), F32),
                        pltpu.VMEM((2, groups, 1, ROW_TILE), F32),
                        pltpu.VMEM((groups, 1, ROW_TILE), F32),
                        pltpu.VMEM((groups, 1, ROW_TILE), F32),
                        pltpu.VMEM((groups, hd, ROW_TILE), F32)],
        compiler_params=_params(("arbitrary",) * 3, blocks, temps),
        name="attention",
    )(proj, keys, vt, proj, cos, sin, _gain_rows(q_g), _pair_swap_matrix(bw))


def _ret_chunk(q_ref, k_ref, v_ref, cos_ref, sin_ref, dec_ref, swap_ref, s_ref, *, qk_dim, hps, reverse):
    tc = ROW_TILE
    v_dim = 2 * qk_dim
    nt = (((1,), (1,)), ((), ()))
    cos, sin = cos_ref[...], sin_ref[...]
    ii = lax.broadcasted_iota(jnp.int32, (tc, tc), 0)
    jj = lax.broadcasted_iota(jnp.int32, (tc, tc), 1)
    pos = lax.broadcasted_iota(jnp.int32, (tc, 1), 0).astype(F32)
    if reverse:
        dist, keep = (jj - ii), jj > ii
        q_pow, k_pow = tc - pos, pos
    else:
        dist, keep = (ii - jj), ii >= jj
        q_pow, k_pow = pos + 1.0, tc - 1.0 - pos
    dist = jnp.maximum(dist, 0).astype(F32)

    raw = []
    for h in range(hps):
        qb = q_ref[:, h * qk_dim:(h + 1) * qk_dim]
        kb = k_ref[:, h * qk_dim:(h + 1) * qk_dim]
        raw.append((qb, kb, _swap_pairs(qb, swap_ref), _swap_pairs(kb, swap_ref)))

    heads = []
    for h, (qb, kb, q_partner, k_partner) in enumerate(raw):
        log_g = jnp.log1p(-jnp.exp(dec_ref[h]))[0:1, 0:1]
        q = qb.astype(F32) * cos + q_partner * sin
        k = (kb.astype(F32) * cos + k_partner * sin) * (qk_dim ** -0.5)
        scores = lax.dot_general(q.astype(BF16), k.astype(BF16), nt, preferred_element_type=F32)
        heads.append((log_g, q, k, scores))

    outs = []
    for h, (log_g, q, k, scores) in enumerate(heads):
        v = v_ref[:, h * v_dim:(h + 1) * v_dim]
        decay = jnp.where(keep, jnp.exp(dist * log_g), 0.0)
        lhs = jnp.concatenate([(scores * decay).astype(BF16),
                               (q * jnp.exp(q_pow * log_g)).astype(BF16)], axis=1)
        rhs = jnp.concatenate([v, s_ref[h].astype(BF16)], axis=0)
        outs.append(jnp.dot(lhs, rhs, preferred_element_type=F32))

    for h, (log_g, q, k, scores) in enumerate(heads):
        v = v_ref[:, h * v_dim:(h + 1) * v_dim]
        kt = (k * jnp.exp(k_pow * log_g)).T.astype(BF16)
        s_ref[h] = s_ref[h] * jnp.exp(tc * log_g) + jnp.dot(kt, v, preferred_element_type=F32)
    return outs


def _ret_fwd_kernel(q_ref, k_ref, v_ref, cos_ref, sin_ref, dec_ref, swap_ref, o_ref, s_ref, *, qk_dim, hps):
    @pl.when(pl.program_id(2) == 0)
    def _():
        s_ref[...] = jnp.zeros(s_ref.shape, F32)

    outs = _ret_chunk(q_ref, k_ref, v_ref, cos_ref, sin_ref, dec_ref, swap_ref, s_ref,
                      qk_dim=qk_dim, hps=hps, reverse=False)
    o_ref[...] = jnp.concatenate(outs, axis=1).astype(o_ref.dtype)


def _ret_bwd_kernel(q_ref, k_ref, v_ref, cos_ref, sin_ref, dec_ref, swap_ref, of_ref, g_ref, gn_ref, o_ref, s_ref,
                    *, qk_dim, hps):
    @pl.when(pl.program_id(2) == 0)
    def _():
        s_ref[...] = jnp.zeros(s_ref.shape, F32)

    outs = _ret_chunk(q_ref, k_ref, v_ref, cos_ref, sin_ref, dec_ref, swap_ref, s_ref,
                      qk_dim=qk_dim, hps=hps, reverse=True)
    v_dim = 2 * qk_dim
    normed = []
    for h, o in enumerate(outs):
        cols = slice(h * v_dim, (h + 1) * v_dim)
        o = o + of_ref[:, cols].astype(F32)
        mu = jnp.mean(o, axis=-1, keepdims=True)
        dev = o - mu
        var = jnp.mean(dev * dev, axis=-1, keepdims=True)
        normed.append(dev * lax.rsqrt(var + GN_EPS) * gn_ref[:, cols])
    o_ref[...] = (jnp.concatenate(normed, axis=1) * _silu(g_ref[...].astype(F32))).astype(o_ref.dtype)


def _retention(proj, cos, sin, decay_f, decay_b, gn_g, *, batch, tiles, ctx_tiles, d, heads):
    r = proj.shape[0]
    qk_dim = d // heads
    v_dim = 2 * qk_dim
    tc = ROW_TILE
    dec_f = jnp.broadcast_to(decay_f.astype(F32)[:, None, None], (heads, 8, 128))
    dec_b = jnp.broadcast_to(decay_b.astype(F32)[:, None, None], (heads, 8, 128))
    hps = _largest_tile(heads, (RET_HEADS_PER_STEP, 1))
    swap = _pair_swap_matrix(qk_dim)
    qw, vw = hps * qk_dim, hps * v_dim
    k_col0 = d // qw
    v_col0 = 2 * d // vw
    g_col0 = (2 * d + heads * v_dim) // vw
    blocks = 2 * tc * qw * 2 + 4 * tc * vw * 2 + 2 * tc * qk_dim * 4 + hps * 8 * 128 * 4 + vw * 4
    temps = hps * (qk_dim * v_dim * 4 + 12 * tc * v_dim * 4)

    def fwd_row(b, h, c):
        return b * tiles + c

    def bwd_chunk(c):
        return jnp.where(c < ctx_tiles, ctx_tiles - 1 - c, tiles - 1 - (c - ctx_tiles))

    def bwd_row(b, h, c):
        return b * tiles + bwd_chunk(c)

    o_f = pl.pallas_call(
        functools.partial(_ret_fwd_kernel, qk_dim=qk_dim, hps=hps),
        grid=(batch, heads // hps, tiles),
        in_specs=[pl.BlockSpec((tc, qw), lambda b, h, c: (fwd_row(b, h, c), h)),
                  pl.BlockSpec((tc, qw), lambda b, h, c: (fwd_row(b, h, c), k_col0 + h)),
                  pl.BlockSpec((tc, vw), lambda b, h, c: (fwd_row(b, h, c), v_col0 + h)),
                  pl.BlockSpec((tc, qk_dim), lambda b, h, c: (c, 0)),
                  pl.BlockSpec((tc, qk_dim), lambda b, h, c: (c, 0)),
                  pl.BlockSpec((hps, 8, 128), lambda b, h, c: (h, 0, 0)),
                  pl.BlockSpec((qk_dim, qk_dim), lambda b, h, c: (0, 0))],
        out_specs=pl.BlockSpec((tc, vw), lambda b, h, c: (fwd_row(b, h, c), h)),
        out_shape=jax.ShapeDtypeStruct((r, heads * v_dim), BF16),
        scratch_shapes=[pltpu.VMEM((hps, qk_dim, v_dim), F32)],
        compiler_params=_params(("arbitrary",) * 3, blocks, temps),
        name="retention_fwd",
    )(proj, proj, proj, cos, sin, dec_f, swap)

    return pl.pallas_call(
        functools.partial(_ret_bwd_kernel, qk_dim=qk_dim, hps=hps),
        grid=(batch, heads // hps, tiles),
        in_specs=[pl.BlockSpec((tc, qw), lambda b, h, c: (bwd_row(b, h, c), h)),
                  pl.BlockSpec((tc, qw), lambda b, h, c: (bwd_row(b, h, c), k_col0 + h)),
                  pl.BlockSpec((tc, vw), lambda b, h, c: (bwd_row(b, h, c), v_col0 + h)),
                  pl.BlockSpec((tc, qk_dim), lambda b, h, c: (bwd_chunk(c), 0)),
                  pl.BlockSpec((tc, qk_dim), lambda b, h, c: (bwd_chunk(c), 0)),
                  pl.BlockSpec((hps, 8, 128), lambda b, h, c: (h, 0, 0)),
                  pl.BlockSpec((qk_dim, qk_dim), lambda b, h, c: (0, 0)),
                  pl.BlockSpec((tc, vw), lambda b, h, c: (bwd_row(b, h, c), h)),
                  pl.BlockSpec((tc, vw), lambda b, h, c: (bwd_row(b, h, c), g_col0 + h)),
                  pl.BlockSpec((1, vw), lambda b, h, c: (0, h))],
        out_specs=pl.BlockSpec((tc, vw), lambda b, h, c: (bwd_row(b, h, c), h)),
        out_shape=jax.ShapeDtypeStruct((r, heads * v_dim), BF16),
        scratch_shapes=[pltpu.VMEM((hps, qk_dim, v_dim), F32)],
        compiler_params=_params(("arbitrary",) * 3, blocks, temps),
        name="retention_bwd_norm_gate",
    )(proj, proj, proj, cos, sin, dec_b, swap, o_f, proj, gn_g.astype(F32).reshape(1, heads * v_dim))


def _final_norm_kernel(x_ref, g_ref, o_ref):
    x = x_ref[...]
    o_ref[...] = x * lax.rsqrt(jnp.mean(x * x, axis=-1, keepdims=True) + NORM_EPS) * g_ref[...]


def _final_norm(xs, g, *, batch, tiles, ctx_tiles):
    d = xs.shape[1]
    seq_tiles = tiles - ctx_tiles
    blocks = 2 * ROW_TILE * d * 4 + d * 4
    return pl.pallas_call(
        _final_norm_kernel,
        grid=(batch, seq_tiles),
        in_specs=[pl.BlockSpec((ROW_TILE, d), lambda b, t: (b * tiles + ctx_tiles + t, 0)),
                  pl.BlockSpec((1, d), lambda b, t: (0, 0))],
        out_specs=pl.BlockSpec((ROW_TILE, d), lambda b, t: (b * seq_tiles + t, 0)),
        out_shape=jax.ShapeDtypeStruct((batch * seq_tiles * ROW_TILE, d), F32),
        compiler_params=_params(("arbitrary", "arbitrary"), blocks, 2 * ROW_TILE * d * 4),
        name="final_norm",
    )(xs, g.reshape(1, d))


def kernel(x, c, ctx, c_ctx, norm_g, mod_w, mod_b, pool_w_in, pool_w_grp, pool_scale, pool_w_out,
           attn_w_in, attn_q_norm, attn_k_norm, attn_w_out, ret_w_in, ret_decay_fwd, ret_decay_bwd,
           ret_gn_g, ret_w_out, final_norm_g):
    batch, seq, d = x.shape
    n_ctx = ctx.shape[1]
    depth = norm_g.shape[0]
    assert seq % ROW_TILE == 0 and n_ctx % ROW_TILE == 0 and seq % GRID_W == 0
    assert batch < MOD_ROWS
    ctx_tiles = n_ctx // ROW_TILE
    tiles = (n_ctx + seq) // ROW_TILE
    rows_per_sample = n_ctx + seq
    tm = _largest_tile(rows_per_sample, (768, 512, 256))

    cvec =jnp.concatenate([c, c_ctx[None, :], jnp.zeros((MOD_ROWS - batch - 1, d), F32)], axis=0)
    mods = _mod_tables(cvec, mod_w, mod_b)
    geom = dict(batch=batch, tiles=tiles, ctx_tiles=ctx_tiles)

    kv_width = (attn_w_in.shape[2] - 2 * d) // 2
    ret_heads = ret_decay_fwd.shape[1]
    attn_cos, attn_sin = _rope_tables(n_ctx, seq, ATTN_HEAD_DIM)
    ret_cos, ret_sin = _rope_tables(n_ctx, seq, d // ret_heads)

    pool_w_out, attn_w_out, ret_w_out = (w.astype(BF16) for w in (pool_w_out, attn_w_out, ret_w_out))
    for i in range(depth):
        kind, j = i % 3, i // 3
        if i == 0:
            h, xs = _join_norm_mod(ctx, x, norm_g[i], mods[i], **geom)
        else:
            h = _norm_mod(xs, norm_g[i], mods[i], **geom)
        if kind == 0:
            width = pool_w_in.shape[2] // 2
            u = _matmul(h, _fold_group_mix(pool_w_in, pool_w_grp, pool_scale, j)[None], 0, name="in_proj_mixed")
            g = _matmul(h, pool_w_in, j, col0=width, n_cols=width)
            z = _pool_mix(u, g, n_groups=pool_w_grp.shape[1], **geom)
            w_out = pool_w_out
        elif kind == 1:
            proj = _matmul(h, attn_w_in, j)
            z = _attention(proj, attn_cos, attn_sin, attn_q_norm[j], attn_k_norm[j],
                           **geom, d=d, kv_width=kv_width)
            w_out = attn_w_out
        else:
            proj = _matmul(h, ret_w_in, j)
            z = _retention(proj, ret_cos, ret_sin, ret_decay_fwd[j], ret_decay_bwd[j], ret_gn_g[j],
                           **geom, d=d, heads=ret_heads)
            w_out = ret_w_out
        xs = _matmul_residual(z, w_out, j, xs, mods[i], batch=batch, tm=tm, ctx_rows=n_ctx)

    out = _final_norm(xs, final_norm_g, **geom)
    return out.reshape(batch, seq, d)
```

```python
import functools
import math

import jax
import jax.numpy as jnp
from jax import lax
from jax.experimental import pallas as pl
from jax.experimental.pallas import tpu as pltpu

F32 = jnp.float32
BF16 = jnp.bfloat16

GRID_W = 64
ATTN_HEAD_DIM = 128
POOL_WINDOWS = (2, 4, 8, 16)
ROPE_THETA = 10000.0
NORM_EPS = 1e-6
GN_EPS = 1e-5

RET_HEADS_PER_STEP = 8
ATTN_SCORES_AHEAD = 2
ROW_TILE = 256
HALO_ROWS = 16
EDGE = max(POOL_WINDOWS) // 2
MOD_ROWS = 8
V7X_VMEM_BYTES = 64 * 1024 * 1024
VMEM_CAP_BYTES = V7X_VMEM_BYTES - 8 * 1024 * 1024


def _vmem_limit(block_bytes, temp_bytes=0):
    need = 2 * block_bytes + temp_bytes + 4 * 1024 * 1024
    return int(min(max(need, 16 * 1024 * 1024), VMEM_CAP_BYTES))


def _params(semantics, block_bytes, temp_bytes=0):
    return pltpu.CompilerParams(dimension_semantics=semantics,
                                vmem_limit_bytes=_vmem_limit(block_bytes, temp_bytes))


def _largest_tile(n, candidates):
    for c in candidates:
        if n % c == 0:
            return c
    raise ValueError(f"no tile among {candidates} divides {n}")


def _silu(v):
    return v * (1.0 / (1.0 + jnp.exp(-v)))


def _mod_kernel(c_ref, w_ref, b_ref, o_ref):
    a = _silu(c_ref[...]).astype(BF16)
    w = w_ref[0].astype(BF16)
    o_ref[0] = jnp.dot(a, w, preferred_element_type=F32) + b_ref[0]


def _mod_tables(cvec, mod_w, mod_b):
    depth, d, n = mod_w.shape
    tn = _largest_tile(n, (512, 256, 128))
    blocks = MOD_ROWS * d * 4 + d * tn * 4 + tn * 4 + MOD_ROWS * tn * 4
    return pl.pallas_call(
        _mod_kernel,
        grid=(depth, n // tn),
        in_specs=[pl.BlockSpec((MOD_ROWS, d), lambda l, j: (0, 0)),
                  pl.BlockSpec((1, d, tn), lambda l, j: (l, 0, j)),
                  pl.BlockSpec((1, 1, tn), lambda l, j: (l, 0, j))],
        out_specs=pl.BlockSpec((1, MOD_ROWS, tn), lambda l, j: (l, 0, j)),
        out_shape=jax.ShapeDtypeStruct((depth, MOD_ROWS, n), F32),
        compiler_params=_params(("arbitrary", "arbitrary"), blocks, d * tn * 2),
        name="adaln_tables",
    )(cvec, mod_w, mod_b.reshape(depth, 1, n))


def _norm_mod_kernel(x_ref, g_ref, mod_ref, o_ref, *, d, ctx_tiles, ctx_row):
    b, t = pl.program_id(0), pl.program_id(1)
    row = jnp.where(t < ctx_tiles, ctx_row, b)
    shift = mod_ref[pl.ds(row, 1), 0:d]
    scale = mod_ref[pl.ds(row, 1), d:2 * d]
    x = x_ref[...]
    y = x * lax.rsqrt(jnp.mean(x * x, axis=-1, keepdims=True) + NORM_EPS) * g_ref[...]
    o_ref[...] = (y * (1.0 + scale) + shift).astype(o_ref.dtype)


def _join_norm_mod_kernel(ctx_ref, x_ref, g_ref, mod_ref, o_ref, xs_ref, *, d, ctx_tiles, ctx_row):
    b, t = pl.program_id(0), pl.program_id(1)

    def emit(x, row):
        shift = mod_ref[pl.ds(row, 1), 0:d]
        scale = mod_ref[pl.ds(row, 1), d:2 * d]
        y = x * lax.rsqrt(jnp.mean(x * x, axis=-1, keepdims=True) + NORM_EPS) * g_ref[...]
        o_ref[...] = (y * (1.0 + scale) + shift).astype(o_ref.dtype)
        xs_ref[...] = x

    @pl.when(t < ctx_tiles)
    def _():
        emit(ctx_ref[...], ctx_row)

    @pl.when(t >= ctx_tiles)
    def _():
        emit(x_ref[...], b)


def _join_norm_mod(ctx, x, g, mods, *, batch, tiles, ctx_tiles):
    d = x.shape[-1]
    seq_tiles = tiles - ctx_tiles
    r = batch * tiles * ROW_TILE
    blocks = 3 * ROW_TILE * d * 4 + d * 4 + MOD_ROWS * 3 * d * 4 + ROW_TILE * d * 2
    return pl.pallas_call(
        functools.partial(_join_norm_mod_kernel, d=d, ctx_tiles=ctx_tiles, ctx_row=batch),
        grid=(batch, tiles),
        in_specs=[pl.BlockSpec((ROW_TILE, d), lambda b, t: (b * ctx_tiles + jnp.minimum(t, ctx_tiles - 1), 0)),
                  pl.BlockSpec((ROW_TILE, d), lambda b, t: (b * seq_tiles + jnp.maximum(t - ctx_tiles, 0), 0)),
                  pl.BlockSpec((1, d), lambda b, t: (0, 0)),
                  pl.BlockSpec((MOD_ROWS, 3 * d), lambda b, t: (0, 0))],
        out_specs=[pl.BlockSpec((ROW_TILE, d), lambda b, t: (b * tiles + t, 0)),
                   pl.BlockSpec((ROW_TILE, d), lambda b, t: (b * tiles + t, 0))],
        out_shape=[jax.ShapeDtypeStruct((r, d), BF16), jax.ShapeDtypeStruct((r, d), F32)],
        compiler_params=_params(("arbitrary", "arbitrary"), blocks, 3 * ROW_TILE * d * 4),
        name="join_norm_modulate",
    )(ctx.reshape(-1, d), x.reshape(-1, d), g.reshape(1, d), mods)


def _norm_mod(xs, g, mods, *, batch, tiles, ctx_tiles):
    r, d = xs.shape
    blocks = ROW_TILE * d * 4 + d * 4 + MOD_ROWS * 3 * d * 4 + ROW_TILE * d * 2
    return pl.pallas_call(
        functools.partial(_norm_mod_kernel, d=d, ctx_tiles=ctx_tiles, ctx_row=batch),
        grid=(batch, tiles),
        in_specs=[pl.BlockSpec((ROW_TILE, d), lambda b, t: (b * tiles + t, 0)),
                  pl.BlockSpec((1, d), lambda b, t: (0, 0)),
                  pl.BlockSpec((MOD_ROWS, 3 * d), lambda b, t: (0, 0))],
        out_specs=pl.BlockSpec((ROW_TILE, d), lambda b, t: (b * tiles + t, 0)),
        out_shape=jax.ShapeDtypeStruct((r, d), BF16),
        compiler_params=_params(("arbitrary", "arbitrary"), blocks, 3 * ROW_TILE * d * 4),
        name="norm_modulate",
    )(xs, g.reshape(1, d), mods)


def _matmul_kernel(a_ref, b_ref, o_ref):
    o_ref[...] = jnp.dot(a_ref[...], b_ref[...].astype(BF16), preferred_element_type=F32).astype(o_ref.dtype)


def _matmul(a, w, layer, *, col0=0, n_cols=None, name="in_proj"):
    m, k = a.shape
    n_cols = w.shape[2] - col0 if n_cols is None else n_cols
    wbytes = jnp.dtype(w.dtype).itemsize
    tn = _largest_tile(math.gcd(n_cols, col0) if col0 else n_cols, (1024, 512, 256, 128) if wbytes == 2 else (512, 256, 128))
    for tm in (1536, 1024, 768, 512, 256):
        blocks = tm * k * 2 + k * tn * wbytes + tm * tn * 2
        temps = tm * tn * (4 + 2) + (k * tn * 2 if wbytes == 4 else 0)
        if m % tm == 0 and 2 * blocks + temps + 4 * 1024 * 1024 <= VMEM_CAP_BYTES:
            break
    else:
        raise ValueError("in-projection blocks do not fit VMEM")
    return pl.pallas_call(
        _matmul_kernel,
        grid=(m // tm, n_cols // tn),
        in_specs=[pl.BlockSpec((tm, k), lambda i, j: (i, 0)),
                  pl.BlockSpec((None, k, tn), lambda i, j: (layer, 0, col0 // tn + j))],
        out_specs=pl.BlockSpec((tm, tn), lambda i, j: (i, j)),
        out_shape=jax.ShapeDtypeStruct((m, n_cols), BF16),
        compiler_params=_params(("arbitrary", "arbitrary"), blocks, temps),
        name=name,
    )(a, w)


def _matmul_res_kernel(a_ref, b_ref, x_ref, gate_ref, o_ref, *, tm, ctx_rows, ctx_row):
    b, t = pl.program_id(0), pl.program_id(1)
    y = jnp.dot(a_ref[...], b_ref[...].astype(BF16), preferred_element_type=F32)
    gate_b = gate_ref[pl.ds(b, 1), :]
    gate_c = gate_ref[pl.ds(ctx_row, 1), :]
    rows = t * tm + lax.broadcasted_iota(jnp.int32, (tm, 1), 0)
    gate = jnp.where(rows < ctx_rows, gate_c, gate_b)
    o_ref[...] = x_ref[...] + gate * y


def _matmul_residual(z, w, layer, xs, mods, *, batch, tm, ctx_rows):
    m, kdim = z.shape
    d = w.shape[2]
    wbytes = jnp.dtype(w.dtype).itemsize
    for tn in (1024, 512, 256, 128):
        blocks = tm * kdim * 2 + kdim * tn * wbytes + 2 * tm * tn * 4 + MOD_ROWS * tn * 4
        temps = 2 * tm * tn * 4 + (kdim * tn * 2 if wbytes == 4 else 0)
        if d % tn == 0 and 2 * blocks + temps + 4 * 1024 * 1024 <= VMEM_CAP_BYTES:
            break
    else:
        raise ValueError("out-projection blocks do not fit VMEM")
    tiles = m // (batch * tm)
    gate_col0 = 2 * d // tn
    return pl.pallas_call(
        functools.partial(_matmul_res_kernel, tm=tm, ctx_rows=ctx_rows, ctx_row=batch),
        grid=(batch, tiles, d // tn),
        in_specs=[pl.BlockSpec((tm, kdim), lambda b, t, j: (b * tiles + t, 0)),
                  pl.BlockSpec((None, kdim, tn), lambda b, t, j: (layer, 0, j)),
                  pl.BlockSpec((tm, tn), lambda b, t, j: (b * tiles + t, j)),
                  pl.BlockSpec((MOD_ROWS, tn), lambda b, t, j: (0, gate_col0 + j))],
        out_specs=pl.BlockSpec((tm, tn), lambda b, t, j: (b * tiles + t, j)),
        out_shape=jax.ShapeDtypeStruct((m, d), F32),
        input_output_aliases={2: 0},
        compiler_params=_params(("arbitrary",) * 3, blocks, temps),
        name="out_proj_residual",
    )(z, w, xs, mods)


def _edge_window_sums(x, window):
    half = window // 2
    total = x[EDGE - half:2 * EDGE - half, :]
    for off in range(-half + 1, window - half):
        total = total + x[EDGE + off:2 * EDGE + off, :]
    return total


def _pool_kernel(up_ref, u_ref, un_ref, g_ref, o_ref, *,
                 ctx_tiles, tiles, ctx_rows, seq_rows):
    t = pl.program_id(1)
    tr = ROW_TILE
    gdim = u_ref.shape[1] // len(POOL_WINDOWS)
    in_ctx = t < ctx_tiles
    first = (t == 0) | (t == ctx_tiles)
    last = (t == ctx_tiles - 1) | (t == tiles - 1)
    seq_tile = jnp.where(in_ctx, t, t - ctx_tiles)
    n_seq = jnp.where(in_ctx, ctx_rows, seq_rows)
    pos = seq_tile * tr + lax.broadcasted_iota(jnp.int32, (tr, 1), 0)
    delta = lax.broadcasted_iota(jnp.int32, (tr, tr), 1) - lax.broadcasted_iota(jnp.int32, (tr, tr), 0)

    for gidx, window in enumerate(POOL_WINDOWS):
        cols = slice(gidx * gdim, (gidx + 1) * gdim)
        half = window // 2
        ub = u_ref[:, cols]
        above = jnp.where(first, 0.0, up_ref[HALO_ROWS - EDGE:HALO_ROWS, cols].astype(F32))
        below = jnp.where(last, 0.0, un_ref[0:EDGE, cols].astype(F32))
        u_head = ub[0:2 * EDGE, :].astype(F32)
        u_tail = ub[tr - 2 * EDGE:tr, :].astype(F32)
        band = (jnp.where((delta >= -half) & (delta < window - half), 1.0 / window, 0.0)
                - jnp.where(delta == 0, 1.0, 0.0)).astype(BF16)
        inner = jnp.dot(band, ub, preferred_element_type=F32)
        cnt = jnp.minimum(pos + (window - half), n_seq) - jnp.maximum(pos - half, 0)
        inv = 1.0 / cnt.astype(F32)
        head = (_edge_window_sums(jnp.concatenate([above, u_head], axis=0), window) * inv[0:EDGE, :]
                - u_head[0:EDGE, :])
        tail = (_edge_window_sums(jnp.concatenate([u_tail, below], axis=0), window) * inv[tr - EDGE:tr, :]
                - u_tail[EDGE:2 * EDGE, :])
        mixed = jnp.concatenate([head, inner[EDGE:tr - EDGE, :], tail], axis=0)
        o_ref[:, cols] = (mixed * _silu(g_ref[:, cols].astype(F32))).astype(o_ref.dtype)


def _fold_kernel(a_ref, b_ref, scale_ref, o_ref, bw_ref):
    @pl.when(pl.program_id(1) == 0)
    def _():
        bw_ref[...] = b_ref[...].astype(BF16)

    mixed = jnp.dot(a_ref[...].astype(BF16), bw_ref[...], preferred_element_type=F32)
    o_ref[...] = (mixed * scale_ref[...]).astype(o_ref.dtype)


def _fold_group_mix(w_in, w_grp, scale, layer):
    d = w_in.shape[1]
    n_groups, gdim = w_grp.shape[1], w_grp.shape[2]
    tm = _largest_tile(d, (512, 256, 128))
    blocks = tm * gdim * 4 + gdim * gdim * 4 + tm * gdim * 2
    return pl.pallas_call(
        _fold_kernel,
        grid=(n_groups, d // tm),
        in_specs=[pl.BlockSpec((None, tm, gdim), lambda g, i: (layer, i, g)),
                  pl.BlockSpec((None, None, gdim, gdim), lambda g, i: (layer, g, 0, 0)),
                  pl.BlockSpec((None, 1, gdim), lambda g, i: (layer, 0, g))],
        out_specs=pl.BlockSpec((tm, gdim), lambda g, i: (i, g)),
        out_shape=jax.ShapeDtypeStruct((d, n_groups * gdim), BF16),
        scratch_shapes=[pltpu.VMEM((gdim, gdim), BF16)],
        compiler_params=_params(("arbitrary", "arbitrary"), blocks, gdim * gdim * 2 + 2 * tm * gdim * 4),
        name="fold_group_mix",
    )(w_in, w_grp, scale.reshape(scale.shape[0], 1, scale.shape[1]))


def _pool_mix(u, g, *, n_groups, batch, tiles, ctx_tiles):
    r, width = u.shape
    gdim = width // n_groups
    tr = ROW_TILE
    hb = tr // HALO_ROWS
    n_halo_blocks = r // HALO_ROWS
    kern = functools.partial(_pool_kernel, ctx_tiles=ctx_tiles, tiles=tiles,
                             ctx_rows=ctx_tiles * tr, seq_rows=(tiles - ctx_tiles) * tr)
    assert n_groups == len(POOL_WINDOWS)
    blocks = 3 * tr * width * 2 + 2 * HALO_ROWS * width * 2
    return pl.pallas_call(
        kern,
        grid=(batch, tiles),
        in_specs=[pl.BlockSpec((HALO_ROWS, width), lambda b, t: (jnp.maximum((b * tiles + t) * hb - 1, 0), 0)),
                  pl.BlockSpec((tr, width), lambda b, t: (b * tiles + t, 0)),
                  pl.BlockSpec((HALO_ROWS, width),
                               lambda b, t: (jnp.minimum((b * tiles + t + 1) * hb, n_halo_blocks - 1), 0)),
                  pl.BlockSpec((tr, width), lambda b, t: (b * tiles + t, 0))],
        out_specs=pl.BlockSpec((tr, width), lambda b, t: (b * tiles + t, 0)),
        out_shape=jax.ShapeDtypeStruct((r, width), BF16),
        compiler_params=_params(("arbitrary",) * 2, blocks, 10 * tr * gdim * 4),
        name="pool_mix",
    )(u, u, u, g)


def _rope_tables(n_ctx, n_seq, head_dim):
    rows = n_seq // GRID_W
    row = jnp.repeat(jnp.arange(rows, dtype=F32), GRID_W)
    col = jnp.tile(jnp.arange(GRID_W, dtype=F32), rows)
    n_freq = head_dim // 4
    inv = ROPE_THETA ** (-jnp.arange(n_freq, dtype=F32) / n_freq)
    ang = jnp.concatenate([row[:, None] * inv, col[:, None] * inv], axis=-1)
    ang = jnp.concatenate([jnp.zeros((n_ctx, head_dim // 2), F32), ang], axis=0)
    cos = jnp.repeat(jnp.cos(ang), 2, axis=-1)
    sin = jnp.repeat(jnp.sin(ang), 2, axis=-1)
    sign = jnp.tile(jnp.array([-1.0, 1.0], F32), head_dim // 2)
    return cos, sin * sign


def _pair_swap_matrix(n):
    idx = jnp.arange(n)
    return (idx[:, None] == (idx[None, :] ^ 1)).astype(BF16)


def _swap_pairs(x_bf16, swap_ref):
    return jnp.dot(x_bf16, swap_ref[...], preferred_element_type=F32)


def _norm_rope(xb, cos, sin, gain, gain_swapped, swap_ref, out_scale):
    hd = ATTN_HEAD_DIM
    x = xb.astype(F32)
    partner = _swap_pairs(xb, swap_ref)
    outs = []
    for h in range(xb.shape[1] // hd):
        cols = slice(h * hd, (h + 1) * hd)
        xh = x[:, cols]
        inv = lax.rsqrt(jnp.mean(xh * xh, axis=-1, keepdims=True) + NORM_EPS) * out_scale
        outs.append(xh * inv * gain * cos + partner[:, cols] * inv * gain_swapped * sin)
    return outs[0] if len(outs) == 1 else jnp.concatenate(outs, axis=1)


def _gain_rows(g):
    hd = ATTN_HEAD_DIM
    swapped = g.reshape(hd // 2, 2)[:, ::-1].reshape(hd)
    return jnp.concatenate([g[None], swapped[None], jnp.zeros((MOD_ROWS - 2, hd), g.dtype)], axis=0).astype(F32)


def _kv_prep_kernel(k_ref, v_ref, cos_ref, sin_ref, g_ref, swap_ref, ko_ref, vt_ref):
    k = _norm_rope(k_ref[...], cos_ref[...], sin_ref[...], g_ref[0:1, :], g_ref[1:2, :], swap_ref, 1.0)
    ko_ref[...] = k.astype(ko_ref.dtype)
    vt_ref[...] = v_ref[...].astype(F32).T.astype(vt_ref.dtype)


def _kv_prep(proj, cos, sin, k_g, *, batch, n_chunks, kv_chunk, d, kv_width):
    hd = ATTN_HEAD_DIM
    kv_heads = kv_width // hd
    k_col0 = d // hd
    v_col0 = (d + kv_width) // hd
    blocks = 4 * kv_chunk * hd * 2 + 2 * kv_chunk * hd * 4 + MOD_ROWS * hd * 4 + hd * hd * 2
    return pl.pallas_call(
        _kv_prep_kernel,
        grid=(batch, kv_heads, n_chunks),
        in_specs=[pl.BlockSpec((kv_chunk, hd), lambda b, h, c: (b * n_chunks + c, k_col0 + h)),
                  pl.BlockSpec((kv_chunk, hd), lambda b, h, c: (b * n_chunks + c, v_col0 + h)),
                  pl.BlockSpec((kv_chunk, hd), lambda b, h, c: (c, 0)),
                  pl.BlockSpec((kv_chunk, hd), lambda b, h, c: (c, 0)),
                  pl.BlockSpec((MOD_ROWS, hd), lambda b, h, c: (0, 0)),
                  pl.BlockSpec((hd, hd), lambda b, h, c: (0, 0))],
        out_specs=[pl.BlockSpec((kv_chunk, hd), lambda b, h, c: (b * n_chunks + c, h)),
                   pl.BlockSpec((None, None, None, hd, kv_chunk), lambda b, h, c: (b, h, c, 0, 0))],
        out_shape=[jax.ShapeDtypeStruct((proj.shape[0], kv_width), BF16),
                   jax.ShapeDtypeStruct((batch, kv_heads, n_chunks, hd, kv_chunk), BF16)],
        compiler_params=_params(("arbitrary",) * 3, blocks, 10 * kv_chunk * hd * 4),
        name="kv_norm_rope_transpose",
    )(proj, proj, cos, sin, _gain_rows(k_g), _pair_swap_matrix(hd))


def _attn_kernel(q_ref, k_ref, vt_ref, g_ref, cos_ref, sin_ref, qg_ref, swap_ref, o_ref,
                 qs_ref, s_ref, smax_ref, m_ref, l_ref, acc_ref, *,
                 groups, ctx_tiles, ctx_rows, n_chunks, kv_chunk, q_scale):
    t = pl.program_id(2)
    hd = ATTN_HEAD_DIM
    nt = (((1,), (1,)), ((), ()))

    qs_ref[...] = _norm_rope(q_ref[...], cos_ref[...], sin_ref[...], qg_ref[0:1, :], qg_ref[1:2, :],
                             swap_ref, q_scale).astype(qs_ref.dtype)

    m_ref[...] = jnp.full(m_ref.shape, -jnp.inf, F32)
    l_ref[...] = jnp.zeros(l_ref.shape, F32)
    acc_ref[...] = jnp.zeros(acc_ref.shape, F32)

    def scores(k, slot, n, h):
        s = lax.dot_general(k, qs_ref[:, h * hd:(h + 1) * hd], nt, preferred_element_type=F32)
        s_ref[slot, h, 0:n, :] = s
        smax_ref[slot, h] = jnp.max(s, axis=0, keepdims=True)

    def softmax_pv(vt, slot, n, h):
        s = s_ref[slot, h, 0:n, :]
        m_prev = m_ref[h]
        m_new = jnp.maximum(m_prev, smax_ref[slot, h])
        alpha = jnp.exp2(m_prev - m_new)
        p = jnp.exp2(s - m_new)
        l_ref[h] = alpha * l_ref[h] + jnp.sum(p, axis=0, keepdims=True)
        m_ref[h] = m_new
        acc_ref[h] = alpha * acc_ref[h] + jnp.dot(vt, p.astype(BF16), preferred_element_type=F32)

    def chunk_step(k_next, vt, slot, n):
        ahead = min(ATTN_SCORES_AHEAD, groups)
        if k_next is not None:
            for h in range(ahead):
                scores(k_next, 1 - slot, n, h)
        for h in range(groups):
            softmax_pv(vt, slot, n, h)
            if k_next is not None and h + ahead < groups:
                scores(k_next, 1 - slot, n, h + ahead)

    def k_chunk(c):
        return k_ref[pl.ds(pl.multiple_of(c * kv_chunk, kv_chunk), kv_chunk), :]

    @pl.when(t < ctx_tiles)
    def _():
        for r0 in range(0, ctx_rows, ROW_TILE):
            c, off = divmod(r0, kv_chunk)
            for h in range(groups):
                scores(k_ref[r0:r0 + ROW_TILE, :], 0, ROW_TILE, h)
            chunk_step(None, vt_ref[c, :, off:off + ROW_TILE], 0, ROW_TILE)

    @pl.when(t >= ctx_tiles)
    def _():
        for h in range(groups):
            scores(k_chunk(0), 0, kv_chunk, h)
        pairs = (n_chunks - 1) // 2

        def body(i, carry):
            c = 2 * i
            chunk_step(k_chunk(c + 1), vt_ref[c], 0, kv_chunk)
            chunk_step(k_chunk(c + 2), vt_ref[c + 1], 1, kv_chunk)
            return carry
        lax.fori_loop(0, pairs, body, 0)
        if n_chunks % 2 == 0:
            chunk_step(k_chunk(n_chunks - 1), vt_ref[n_chunks - 2], 0, kv_chunk)
            chunk_step(None, vt_ref[n_chunks - 1], 1, kv_chunk)
        else:
            chunk_step(None, vt_ref[n_chunks - 1], 0, kv_chunk)

    o = jnp.concatenate([(acc_ref[h] * (1.0 / l_ref[h])).T for h in range(groups)], axis=1)
    o_ref[...] = (o * _silu(g_ref[...].astype(F32))).astype(o_ref.dtype)


def _attention(proj, cos, sin, q_g, k_g, *, batch, tiles, ctx_tiles, d, kv_width):
    r = proj.shape[0]
    hd = ATTN_HEAD_DIM
    kv_heads = kv_width // hd
    groups = d // kv_width
    bw = groups * hd
    kv_rows = tiles * ROW_TILE
    kv_chunk = _largest_tile(kv_rows, (2816, 1408, 768, 512, 256))
    n_chunks = kv_rows // kv_chunk
    keys, vt = _kv_prep(proj, cos, sin, k_g, batch=batch, n_chunks=n_chunks, kv_chunk=kv_chunk,
                        d=d, kv_width=kv_width)
    kern = functools.partial(_attn_kernel, groups=groups, ctx_tiles=ctx_tiles, ctx_rows=ctx_tiles * ROW_TILE,
                             n_chunks=n_chunks, kv_chunk=kv_chunk, q_scale=hd ** -0.5 * math.log2(math.e))
    blocks = (3 * ROW_TILE * bw * 2 + 2 * kv_rows * hd * 2 + 2 * ROW_TILE * hd * 4 + MOD_ROWS * hd * 4
              + bw * bw * 2)
    temps = (2 * groups + 6) * kv_chunk * ROW_TILE * 4 + groups * (hd + 16) * ROW_TILE * 4 + 8 * ROW_TILE * bw * 4
    return pl.pallas_call(
        kern,
        grid=(batch, kv_heads, tiles),
        in_specs=[pl.BlockSpec((ROW_TILE, bw), lambda b, h, t: (b * tiles + t, h)),
                  pl.BlockSpec((kv_rows, hd), lambda b, h, t: (b, h)),
                  pl.BlockSpec((None, None, n_chunks, hd, kv_chunk), lambda b, h, t: (b, h, 0, 0, 0)),
                  pl.BlockSpec((ROW_TILE, bw), lambda b, h, t: (b * tiles + t, (d + 2 * kv_width) // bw + h)),
                  pl.BlockSpec((ROW_TILE, hd), lambda b, h, t: (t, 0)),
                  pl.BlockSpec((ROW_TILE, hd), lambda b, h, t: (t, 0)),
                  pl.BlockSpec((MOD_ROWS, hd), lambda b, h, t: (0, 0)),
                  pl.BlockSpec((bw, bw), lambda b, h, t: (0, 0))],
        out_specs=pl.BlockSpec((ROW_TILE, bw), lambda b, h, t: (b * tiles + t, h)),
        out_shape=jax.ShapeDtypeStruct((r, d), BF16),
        scratch_shapes=[pltpu.VMEM((ROW_TILE, bw), BF16),
                        pltpu.VMEM((2, groups, kv_chunk, ROW_TILE), F32),
                        pltpu.VMEM((2, groups, 1, ROW_TILE), F32),
                        pltpu.VMEM((groups, 1, ROW_TILE), F32),
                        pltpu.VMEM((groups, 1, ROW_TILE), F32),
                        pltpu.VMEM((groups, hd, ROW_TILE), F32)],
        compiler_params=_params(("arbitrary",) * 3, blocks, temps),
        name="attention",
    )(proj, keys, vt, proj, cos, sin, _gain_rows(q_g), _pair_swap_matrix(bw))


def _ret_chunk(q_ref, k_ref, v_ref, cos_ref, sin_ref, dec_ref, swap_ref, s_ref, *, qk_dim, hps, reverse):
    tc = ROW_TILE
    v_dim = 2 * qk_dim
    nt = (((1,), (1,)), ((), ()))
    cos, sin = cos_ref[...], sin_ref[...]
    ii = lax.broadcasted_iota(jnp.int32, (tc, tc), 0)
    jj = lax.broadcasted_iota(jnp.int32, (tc, tc), 1)
    pos = lax.broadcasted_iota(jnp.int32, (tc, 1), 0).astype(F32)
    if reverse:
        dist, keep = (jj - ii), jj > ii
        q_pow, k_pow = tc - pos, pos
    else:
        dist, keep = (ii - jj), ii >= jj
        q_pow, k_pow = pos + 1.0, tc - 1.0 - pos
    dist = jnp.maximum(dist, 0).astype(F32)

    raw = []
    for h in range(hps):
        qb = q_ref[:, h * qk_dim:(h + 1) * qk_dim]
        kb = k_ref[:, h * qk_dim:(h + 1) * qk_dim]
        raw.append((qb, kb, _swap_pairs(qb, swap_ref), _swap_pairs(kb, swap_ref)))

    heads = []
    for h, (qb, kb, q_partner, k_partner) in enumerate(raw):
        log_g = jnp.log1p(-jnp.exp(dec_ref[h]))[0:1, 0:1]
        q = qb.astype(F32) * cos + q_partner * sin
        k = (kb.astype(F32) * cos + k_partner * sin) * (qk_dim ** -0.5)
        scores = lax.dot_general(q.astype(BF16), k.astype(BF16), nt, preferred_element_type=F32)
        heads.append((log_g, q, k, scores))

    outs = []
    for h, (log_g, q, k, scores) in enumerate(heads):
        v = v_ref[:, h * v_dim:(h + 1) * v_dim]
        decay = jnp.where(keep, jnp.exp(dist * log_g), 0.0)
        lhs = jnp.concatenate([(scores * decay).astype(BF16),
                               (q * jnp.exp(q_pow * log_g)).astype(BF16)], axis=1)
        rhs = jnp.concatenate([v, s_ref[h].astype(BF16)], axis=0)
        outs.append(jnp.dot(lhs, rhs, preferred_element_type=F32))

    for h, (log_g, q, k, scores) in enumerate(heads):
        v = v_ref[:, h * v_dim:(h + 1) * v_dim]
        kt = (k * jnp.exp(k_pow * log_g)).T.astype(BF16)
        s_ref[h] = s_ref[h] * jnp.exp(tc * log_g) + jnp.dot(kt, v, preferred_element_type=F32)
    return outs


def _ret_fwd_kernel(q_ref, k_ref, v_ref, cos_ref, sin_ref, dec_ref, swap_ref, o_ref, s_ref, *, qk_dim, hps):
    @pl.when(pl.program_id(2) == 0)
    def _():
        s_ref[...] = jnp.zeros(s_ref.shape, F32)

    outs = _ret_chunk(q_ref, k_ref, v_ref, cos_ref, sin_ref, dec_ref, swap_ref, s_ref,
                      qk_dim=qk_dim, hps=hps, reverse=False)
    o_ref[...] = jnp.concatenate(outs, axis=1).astype(o_ref.dtype)


def _ret_bwd_kernel(q_ref, k_ref, v_ref, cos_ref, sin_ref, dec_ref, swap_ref, of_ref, g_ref, gn_ref, o_ref, s_ref,
                    *, qk_dim, hps):
    @pl.when(pl.program_id(2) == 0)
    def _():
        s_ref[...] = jnp.zeros(s_ref.shape, F32)

    outs = _ret_chunk(q_ref, k_ref, v_ref, cos_ref, sin_ref, dec_ref, swap_ref, s_ref,
                      qk_dim=qk_dim, hps=hps, reverse=True)
    v_dim = 2 * qk_dim
    normed = []
    for h, o in enumerate(outs):
        cols = slice(h * v_dim, (h + 1) * v_dim)
        o = o + of_ref[:, cols].astype(F32)
        mu = jnp.mean(o, axis=-1, keepdims=True)
        dev = o - mu
        var = jnp.mean(dev * dev, axis=-1, keepdims=True)
        normed.append(dev * lax.rsqrt(var + GN_EPS) * gn_ref[:, cols])
    o_ref[...] = (jnp.concatenate(normed, axis=1) * _silu(g_ref[...].astype(F32))).astype(o_ref.dtype)


def _retention(proj, cos, sin, decay_f, decay_b, gn_g, *, batch, tiles, ctx_tiles, d, heads):
    r = proj.shape[0]
    qk_dim = d // heads
    v_dim = 2 * qk_dim
    tc = ROW_TILE
    dec_f = jnp.broadcast_to(decay_f.astype(F32)[:, None, None], (heads, 8, 128))
    dec_b = jnp.broadcast_to(decay_b.astype(F32)[:, None, None], (heads, 8, 128))
    hps = _largest_tile(heads, (RET_HEADS_PER_STEP, 1))
    swap = _pair_swap_matrix(qk_dim)
    qw, vw = hps * qk_dim, hps * v_dim
    k_col0 = d // qw
    v_col0 = 2 * d // vw
    g_col0 = (2 * d + heads * v_dim) // vw
    blocks = 2 * tc * qw * 2 + 4 * tc * vw * 2 + 2 * tc * qk_dim * 4 + hps * 8 * 128 * 4 + vw * 4
    temps = hps * (qk_dim * v_dim * 4 + 12 * tc * v_dim * 4)

    def fwd_row(b, h, c):
        return b * tiles + c

    def bwd_chunk(c):
        return jnp.where(c < ctx_tiles, ctx_tiles - 1 - c, tiles - 1 - (c - ctx_tiles))

    def bwd_row(b, h, c):
        return b * tiles + bwd_chunk(c)

    o_f = pl.pallas_call(
        functools.partial(_ret_fwd_kernel, qk_dim=qk_dim, hps=hps),
        grid=(batch, heads // hps, tiles),
        in_specs=[pl.BlockSpec((tc, qw), lambda b, h, c: (fwd_row(b, h, c), h)),
                  pl.BlockSpec((tc, qw), lambda b, h, c: (fwd_row(b, h, c), k_col0 + h)),
                  pl.BlockSpec((tc, vw), lambda b, h, c: (fwd_row(b, h, c), v_col0 + h)),
                  pl.BlockSpec((tc, qk_dim), lambda b, h, c: (c, 0)),
                  pl.BlockSpec((tc, qk_dim), lambda b, h, c: (c, 0)),
                  pl.BlockSpec((hps, 8, 128), lambda b, h, c: (h, 0, 0)),
                  pl.BlockSpec((qk_dim, qk_dim), lambda b, h, c: (0, 0))],
        out_specs=pl.BlockSpec((tc, vw), lambda b, h, c: (fwd_row(b, h, c), h)),
        out_shape=jax.ShapeDtypeStruct((r, heads * v_dim), BF16),
        scratch_shapes=[pltpu.VMEM((hps, qk_dim, v_dim), F32)],
        compiler_params=_params(("arbitrary",) * 3, blocks, temps),
        name="retention_fwd",
    )(proj, proj, proj, cos, sin, dec_f, swap)

    return pl.pallas_call(
        functools.partial(_ret_bwd_kernel, qk_dim=qk_dim, hps=hps),
        grid=(batch, heads // hps, tiles),
        in_specs=[pl.BlockSpec((tc, qw), lambda b, h, c: (bwd_row(b, h, c), h)),
                  pl.BlockSpec((tc, qw), lambda b, h, c: (bwd_row(b, h, c), k_col0 + h)),
                  pl.BlockSpec((tc, vw), lambda b, h, c: (bwd_row(b, h, c), v_col0 + h)),
                  pl.BlockSpec((tc, qk_dim), lambda b, h, c: (bwd_chunk(c), 0)),
                  pl.BlockSpec((tc, qk_dim), lambda b, h, c: (bwd_chunk(c), 0)),
                  pl.BlockSpec((hps, 8, 128), lambda b, h, c: (h, 0, 0)),
                  pl.BlockSpec((qk_dim, qk_dim), lambda b, h, c: (0, 0)),
                  pl.BlockSpec((tc, vw), lambda b, h, c: (bwd_row(b, h, c), h)),
                  pl.BlockSpec((tc, vw), lambda b, h, c: (bwd_row(b, h, c), g_col0 + h)),
                  pl.BlockSpec((1, vw), lambda b, h, c: (0, h))],
        out_specs=pl.BlockSpec((tc, vw), lambda b, h, c: (bwd_row(b, h, c), h)),
        out_shape=jax.ShapeDtypeStruct((r, heads * v_dim), BF16),
        scratch_shapes=[pltpu.VMEM((hps, qk_dim, v_dim), F32)],
        compiler_params=_params(("arbitrary",) * 3, blocks, temps),
        name="retention_bwd_norm_gate",
    )(proj, proj, proj, cos, sin, dec_b, swap, o_f, proj, gn_g.astype(F32).reshape(1, heads * v_dim))


def _final_norm_kernel(x_ref, g_ref, o_ref):
    x = x_ref[...]
    o_ref[...] = x * lax.rsqrt(jnp.mean(x * x, axis=-1, keepdims=True) + NORM_EPS) * g_ref[...]


def _final_norm(xs, g, *, batch, tiles, ctx_tiles):
    d = xs.shape[1]
    seq_tiles = tiles - ctx_tiles
    blocks = 2 * ROW_TILE * d * 4 + d * 4
    return pl.pallas_call(
        _final_norm_kernel,
        grid=(batch, seq_tiles),
        in_specs=[pl.BlockSpec((ROW_TILE, d), lambda b, t: (b * tiles + ctx_tiles + t, 0)),
                  pl.BlockSpec((1, d), lambda b, t: (0, 0))],
        out_specs=pl.BlockSpec((ROW_TILE, d), lambda b, t: (b * seq_tiles + t, 0)),
        out_shape=jax.ShapeDtypeStruct((batch * seq_tiles * ROW_TILE, d), F32),
        compiler_params=_params(("arbitrary", "arbitrary"), blocks, 2 * ROW_TILE * d * 4),
        name="final_norm",
    )(xs, g.reshape(1, d))


def kernel(x, c, ctx, c_ctx, norm_g, mod_w, mod_b, pool_w_in, pool_w_grp, pool_scale, pool_w_out,
           attn_w_in, attn_q_norm, attn_k_norm, attn_w_out, ret_w_in, ret_decay_fwd, ret_decay_bwd,
           ret_gn_g, ret_w_out, final_norm_g):
    batch, seq, d = x.shape
    n_ctx = ctx.shape[1]
    depth = norm_g.shape[0]
    assert seq % ROW_TILE == 0 and n_ctx % ROW_TILE == 0 and seq % GRID_W == 0
    assert batch < MOD_ROWS
    ctx_tiles = n_ctx // ROW_TILE
    tiles = (n_ctx + seq) // ROW_TILE
    rows_per_sample = n_ctx + seq
    tm = _largest_tile(rows_per_sample, (768, 512, 256))

    cvec = jnp.concatenate([c, c_ctx[None, :], jnp.zeros((MOD_ROWS - batch - 1, d), F32)], axis=0)
    mods = _mod_tables(cvec, mod_w, mod_b)
    geom = dict(batch=batch, tiles=tiles, ctx_tiles=ctx_tiles)

    kv_width = (attn_w_in.shape[2] - 2 * d) // 2
    ret_heads = ret_decay_fwd.shape[1]
    attn_cos, attn_sin = _rope_tables(n_ctx, seq, ATTN_HEAD_DIM)
    ret_cos, ret_sin = _rope_tables(n_ctx, seq, d // ret_heads)

    pool_w_out, attn_w_out, ret_w_out = (w.astype(BF16) for w in (pool_w_out, attn_w_out, ret_w_out))
    for i in range(depth):
        kind, j = i % 3, i // 3
        if i == 0:
            h, xs = _join_norm_mod(ctx, x, norm_g[i], mods[i], **geom)
        else:
            h = _norm_mod(xs, norm_g[i], mods[i], **geom)
        if kind == 0:
            width = pool_w_in.shape[2] // 2
            u = _matmul(h, _fold_group_mix(pool_w_in, pool_w_grp, pool_scale, j)[None], 0, name="in_proj_mixed")
            g = _matmul(h, pool_w_in, j, col0=width, n_cols=width)
            z = _pool_mix(u, g, n_groups=pool_w_grp.shape[1], **geom)
            w_out = pool_w_out
        elif kind == 1:
            proj = _matmul(h, attn_w_in, j)
            z = _attention(proj, attn_cos, attn_sin, attn_q_norm[j], attn_k_norm[j],
                           **geom, d=d, kv_width=kv_width)
            w_out = attn_w_out
        else:
            proj = _matmul(h, ret_w_in, j)
            z = _retention(proj, ret_cos, ret_sin, ret_decay_fwd[j], ret_decay_bwd[j], ret_gn_g[j],
                           **geom, d=d, heads=ret_heads)
            w_out = ret_w_out
        xs = _matmul_residual(z, w_out, j, xs, mods[i], batch=batch, tm=tm, ctx_rows=n_ctx)

    out = _final_norm(xs, final_norm_g, **geom)
    return out.reshape(batch, seq, d)
```

```python
import functools
import math

import jax
import jax.numpy as jnp
from jax import lax
from jax.experimental import pallas as pl
from jax.experimental.pallas import tpu as pltpu

F32 = jnp.float32
BF16 = jnp.bfloat16

GRID_W = 64
ATTN_HEAD_DIM = 128
POOL_WINDOWS = (2, 4, 8, 16)
ROPE_THETA = 10000.0
NORM_EPS = 1e-6
GN_EPS = 1e-5

RET_HEADS_PER_STEP = 8
ATTN_UNROLL_PAIRS = 5
ATTN_SCORES_AHEAD = 2
ROW_TILE = 256
HALO_ROWS = 16
EDGE = max(POOL_WINDOWS) // 2
MOD_ROWS = 8
V7X_VMEM_BYTES = 64 * 1024 * 1024
VMEM_CAP_BYTES = V7X_VMEM_BYTES - 8 * 1024 * 1024


def _vmem_limit(block_bytes, temp_bytes=0):
    need = 2 * block_bytes + temp_bytes + 4 * 1024 * 1024
    return int(min(max(need, 16 * 1024 * 1024), VMEM_CAP_BYTES))


def _params(semantics, block_bytes, temp_bytes=0):
    return pltpu.CompilerParams(dimension_semantics=semantics,
                                vmem_limit_bytes=_vmem_limit(block_bytes, temp_bytes))


def _largest_tile(n, candidates):
    for c in candidates:
        if n % c == 0:
            return c
    raise ValueError(f"no tile among {candidates} divides {n}")


def _silu(v):
    return v * (1.0 / (1.0 + jnp.exp(-v)))


def _mod_kernel(c_ref, w_ref, b_ref, o_ref):
    a = _silu(c_ref[...]).astype(BF16)
    w = w_ref[0].astype(BF16)
    o_ref[0] = jnp.dot(a, w, preferred_element_type=F32) + b_ref[0]


def _mod_tables(cvec, mod_w, mod_b):
    depth, d, n = mod_w.shape
    tn = _largest_tile(n, (512, 256, 128))
    blocks = MOD_ROWS * d * 4 + d * tn * 4 + tn * 4 + MOD_ROWS * tn * 4
    return pl.pallas_call(
        _mod_kernel,
        grid=(depth, n // tn),
        in_specs=[pl.BlockSpec((MOD_ROWS, d), lambda l, j: (0, 0)),
                  pl.BlockSpec((1, d, tn), lambda l, j: (l, 0, j)),
                  pl.BlockSpec((1, 1, tn), lambda l, j: (l, 0, j))],
        out_specs=pl.BlockSpec((1, MOD_ROWS, tn), lambda l, j: (l, 0, j)),
        out_shape=jax.ShapeDtypeStruct((depth, MOD_ROWS, n), F32),
        compiler_params=_params(("arbitrary", "arbitrary"), blocks, d * tn * 2),
        name="adaln_tables",
    )(cvec, mod_w, mod_b.reshape(depth, 1, n))


def _norm_mod_kernel(x_ref, g_ref, mod_ref, o_ref, *, d, ctx_tiles, ctx_row):
    b, t = pl.program_id(0), pl.program_id(1)
    row = jnp.where(t < ctx_tiles, ctx_row, b)
    shift = mod_ref[pl.ds(row, 1), 0:d]
    scale = mod_ref[pl.ds(row, 1), d:2 * d]
    x = x_ref[...]
    y = x * lax.rsqrt(jnp.mean(x * x, axis=-1, keepdims=True) + NORM_EPS) * g_ref[...]
    o_ref[...] = (y * (1.0 + scale) + shift).astype(o_ref.dtype)


def _join_norm_mod_kernel(ctx_ref, x_ref, g_ref, mod_ref, o_ref, xs_ref, *, d, ctx_tiles, ctx_row):
    b, t = pl.program_id(0), pl.program_id(1)

    def emit(x, row):
        shift = mod_ref[pl.ds(row, 1), 0:d]
        scale = mod_ref[pl.ds(row, 1), d:2 * d]
        y = x * lax.rsqrt(jnp.mean(x * x, axis=-1, keepdims=True) + NORM_EPS) * g_ref[...]
        o_ref[...] = (y * (1.0 + scale) + shift).astype(o_ref.dtype)
        xs_ref[...] = x

    @pl.when(t < ctx_tiles)
    def _():
        emit(ctx_ref[...], ctx_row)

    @pl.when(t >= ctx_tiles)
    def _():
        emit(x_ref[...], b)


def _join_norm_mod(ctx, x, g, mods, *, batch, tiles, ctx_tiles):
    d = x.shape[-1]
    seq_tiles = tiles - ctx_tiles
    r = batch * tiles * ROW_TILE
    blocks = 3 * ROW_TILE * d * 4 + d * 4 + MOD_ROWS * 3 * d * 4 + ROW_TILE * d * 2
    return pl.pallas_call(
        functools.partial(_join_norm_mod_kernel, d=d, ctx_tiles=ctx_tiles, ctx_row=batch),
        grid=(batch, tiles),
        in_specs=[pl.BlockSpec((ROW_TILE, d), lambda b, t: (b * ctx_tiles + jnp.minimum(t, ctx_tiles - 1), 0)),
                  pl.BlockSpec((ROW_TILE, d), lambda b, t: (b * seq_tiles + jnp.maximum(t - ctx_tiles, 0), 0)),
                  pl.BlockSpec((1, d), lambda b, t: (0, 0)),
                  pl.BlockSpec((MOD_ROWS, 3 * d), lambda b, t: (0, 0))],
        out_specs=[pl.BlockSpec((ROW_TILE, d), lambda b, t: (b * tiles + t, 0)),
                   pl.BlockSpec((ROW_TILE, d), lambda b, t: (b * tiles + t, 0))],
        out_shape=[jax.ShapeDtypeStruct((r, d), BF16), jax.ShapeDtypeStruct((r, d), F32)],
        compiler_params=_params(("arbitrary", "arbitrary"), blocks, 3 * ROW_TILE * d * 4),
        name="join_norm_modulate",
    )(ctx.reshape(-1, d), x.reshape(-1, d), g.reshape(1, d), mods)


def _norm_mod(xs, g, mods, *, batch, tiles, ctx_tiles):
    r, d = xs.shape
    blocks = ROW_TILE * d * 4 + d * 4 + MOD_ROWS * 3 * d * 4 + ROW_TILE * d * 2
    return pl.pallas_call(
        functools.partial(_norm_mod_kernel, d=d, ctx_tiles=ctx_tiles, ctx_row=batch),
        grid=(batch, tiles),
        in_specs=[pl.BlockSpec((ROW_TILE, d), lambda b, t: (b * tiles + t, 0)),
                  pl.BlockSpec((1, d), lambda b, t: (0, 0)),
                  pl.BlockSpec((MOD_ROWS, 3 * d), lambda b, t: (0, 0))],
        out_specs=pl.BlockSpec((ROW_TILE, d), lambda b, t: (b * tiles + t, 0)),
        out_shape=jax.ShapeDtypeStruct((r, d), BF16),
        compiler_params=_params(("arbitrary", "arbitrary"), blocks, 3 * ROW_TILE * d * 4),
        name="norm_modulate",
    )(xs, g.reshape(1, d), mods)


def _matmul_kernel(a_ref, b_ref, o_ref, *, gate=False):
    y = jnp.dot(a_ref[...], b_ref[...].astype(BF16), preferred_element_type=F32)
    o_ref[...] = (_silu(y) if gate else y).astype(o_ref.dtype)


def _matmul(a, w, layer, *, col0=0, n_cols=None, gate=False, name="in_proj"):
    m, k = a.shape
    n_cols = w.shape[2] - col0 if n_cols is None else n_cols
    wbytes = jnp.dtype(w.dtype).itemsize
    tn = _largest_tile(math.gcd(n_cols, col0) if col0 else n_cols, (1024, 512, 256, 128) if wbytes == 2 else (512, 256, 128))
    for tm in (1536, 1024, 768, 512, 256):
        blocks = tm * k * 2 + k * tn * wbytes + tm * tn * 2
        temps = tm * tn * (4 + 2) + (k * tn * 2 if wbytes == 4 else 0)
        if m % tm == 0 and 2 * blocks + temps + 4 * 1024 * 1024 <= VMEM_CAP_BYTES:
            break
    else:
        raise ValueError("in-projection blocks do not fit VMEM")
    return pl.pallas_call(
        functools.partial(_matmul_kernel, gate=gate),
        grid=(m // tm, n_cols // tn),
        in_specs=[pl.BlockSpec((tm, k), lambda i, j: (i, 0)),
                  pl.BlockSpec((None, k, tn), lambda i, j: (layer, 0, col0 // tn + j))],
        out_specs=pl.BlockSpec((tm, tn), lambda i, j: (i, j)),
        out_shape=jax.ShapeDtypeStruct((m, n_cols), BF16),
        compiler_params=_params(("arbitrary", "arbitrary"), blocks, temps),
        name=name,
    )(a, w)


def _matmul_res_kernel(a_ref, b_ref, x_ref, gate_ref, o_ref, *, tm, ctx_rows, ctx_row):
    b, t = pl.program_id(0), pl.program_id(1)
    y = jnp.dot(a_ref[...], b_ref[...].astype(BF16), preferred_element_type=F32)
    gate_b = gate_ref[pl.ds(b, 1), :]
    gate_c = gate_ref[pl.ds(ctx_row, 1), :]
    rows = t * tm + lax.broadcasted_iota(jnp.int32, (tm, 1), 0)
    gate = jnp.where(rows < ctx_rows, gate_c, gate_b)
    o_ref[...] = x_ref[...] + gate * y


def _matmul_residual(z, w, layer, xs, mods, *, batch, tm, ctx_rows):
    m, kdim = z.shape
    d = w.shape[2]
    wbytes = jnp.dtype(w.dtype).itemsize
    for tn in (1024, 512, 256, 128):
        blocks = tm * kdim * 2 + kdim * tn * wbytes + 2 * tm * tn * 4 + MOD_ROWS * tn * 4
        temps = 2 * tm * tn * 4 + (kdim * tn * 2 if wbytes == 4 else 0)
        if d % tn == 0 and 2 * blocks + temps + 4 * 1024 * 1024 <= VMEM_CAP_BYTES:
            break
    else:
        raise ValueError("out-projection blocks do not fit VMEM")
    tiles = m // (batch * tm)
    gate_col0 = 2 * d // tn
    return pl.pallas_call(
        functools.partial(_matmul_res_kernel, tm=tm, ctx_rows=ctx_rows, ctx_row=batch),
        grid=(batch, tiles, d // tn),
        in_specs=[pl.BlockSpec((tm, kdim), lambda b, t, j: (b * tiles + t, 0)),
                  pl.BlockSpec((None, kdim, tn), lambda b, t, j: (layer, 0, j)),
                  pl.BlockSpec((tm, tn), lambda b, t, j: (b * tiles + t, j)),
                  pl.BlockSpec((MOD_ROWS, tn), lambda b, t, j: (0, gate_col0 + j))],
        out_specs=pl.BlockSpec((tm, tn), lambda b, t, j: (b * tiles + t, j)),
        out_shape=jax.ShapeDtypeStruct((m, d), F32),
        input_output_aliases={2: 0},
        compiler_params=_params(("arbitrary",) * 3, blocks, temps),
        name="out_proj_residual",
    )(z, w, xs, mods)


def _edge_window_sums(x, window):
    half = window // 2
    total = x[EDGE - half:2 * EDGE - half, :]
    for off in range(-half + 1, window - half):
        total = total + x[EDGE + off:2 * EDGE + off, :]
    return total


def _pool_kernel(up_ref, u_ref, un_ref, g_ref, o_ref, *,
                 ctx_tiles, tiles, ctx_rows, seq_rows):
    t = pl.program_id(1)
    tr = ROW_TILE
    gdim = u_ref.shape[1] // len(POOL_WINDOWS)
    in_ctx = t < ctx_tiles
    first = (t == 0) | (t == ctx_tiles)
    last = (t == ctx_tiles - 1) | (t == tiles - 1)
    seq_tile = jnp.where(in_ctx, t, t - ctx_tiles)
    n_seq = jnp.where(in_ctx, ctx_rows, seq_rows)
    pos = seq_tile * tr + lax.broadcasted_iota(jnp.int32, (tr, 1), 0)
    delta = lax.broadcasted_iota(jnp.int32, (tr, tr), 1) - lax.broadcasted_iota(jnp.int32, (tr, tr), 0)

    for gidx, window in enumerate(POOL_WINDOWS):
        cols = slice(gidx * gdim, (gidx + 1) * gdim)
        half = window // 2
        ub = u_ref[:, cols]
        above = jnp.where(first, 0.0, up_ref[HALO_ROWS - EDGE:HALO_ROWS, cols].astype(F32))
        below = jnp.where(last, 0.0, un_ref[0:EDGE, cols].astype(F32))
        u_head = ub[0:2 * EDGE, :].astype(F32)
        u_tail = ub[tr - 2 * EDGE:tr, :].astype(F32)
        band = (jnp.where((delta >= -half) & (delta < window - half), 1.0 / window, 0.0)
                - jnp.where(delta == 0, 1.0, 0.0)).astype(BF16)
        inner = jnp.dot(band, ub, preferred_element_type=F32)
        cnt = jnp.minimum(pos + (window - half), n_seq) - jnp.maximum(pos - half, 0)
        inv = 1.0 / cnt.astype(F32)
        head = (_edge_window_sums(jnp.concatenate([above, u_head], axis=0), window) * inv[0:EDGE, :]
                - u_head[0:EDGE, :])
        tail = (_edge_window_sums(jnp.concatenate([u_tail, below], axis=0), window) * inv[tr - EDGE:tr, :]
                - u_tail[EDGE:2 * EDGE, :])
        mixed = jnp.concatenate([head, inner[EDGE:tr - EDGE, :], tail], axis=0)
        o_ref[:, cols] = (mixed * g_ref[:, cols].astype(F32)).astype(o_ref.dtype)


def _fold_kernel(a_ref, b_ref, scale_ref, o_ref, bw_ref):
    @pl.when(pl.program_id(1) == 0)
    def _():
        bw_ref[...] = b_ref[...].astype(BF16)

    mixed = jnp.dot(a_ref[...].astype(BF16), bw_ref[...], preferred_element_type=F32)
    o_ref[...] = (mixed * scale_ref[...]).astype(o_ref.dtype)


def _fold_group_mix(w_in, w_grp, scale, layer):
    d = w_in.shape[1]
    n_groups, gdim = w_grp.shape[1], w_grp.shape[2]
    tm = _largest_tile(d, (512, 256, 128))
    blocks = tm * gdim * 4 + gdim * gdim * 4 + tm * gdim * 2
    return pl.pallas_call(
        _fold_kernel,
        grid=(n_groups, d // tm),
        in_specs=[pl.BlockSpec((None, tm, gdim), lambda g, i: (layer, i, g)),
                  pl.BlockSpec((None, None, gdim, gdim), lambda g, i: (layer, g, 0, 0)),
                  pl.BlockSpec((None, 1, gdim), lambda g, i: (layer, 0, g))],
        out_specs=pl.BlockSpec((tm, gdim), lambda g, i: (i, g)),
        out_shape=jax.ShapeDtypeStruct((d, n_groups * gdim), BF16),
        scratch_shapes=[pltpu.VMEM((gdim, gdim), BF16)],
        compiler_params=_params(("arbitrary", "arbitrary"), blocks, gdim * gdim * 2 + 2 * tm * gdim * 4),
        name="fold_group_mix",
    )(w_in, w_grp, scale.reshape(scale.shape[0], 1, scale.shape[1]))


def _pool_mix(u, g, *, n_groups, batch, tiles, ctx_tiles):
    r, width = u.shape
    gdim = width // n_groups
    tr = ROW_TILE
    hb = tr // HALO_ROWS
    n_halo_blocks = r // HALO_ROWS
    kern = functools.partial(_pool_kernel, ctx_tiles=ctx_tiles, tiles=tiles,
                             ctx_rows=ctx_tiles * tr, seq_rows=(tiles - ctx_tiles) * tr)
    assert n_groups == len(POOL_WINDOWS)
    blocks = 3 * tr * width * 2 + 2 * HALO_ROWS * width * 2
    return pl.pallas_call(
        kern,
        grid=(batch, tiles),
        in_specs=[pl.BlockSpec((HALO_ROWS, width), lambda b, t: (jnp.maximum((b * tiles + t) * hb - 1, 0), 0)),
                  pl.BlockSpec((tr, width), lambda b, t: (b * tiles + t, 0)),
                  pl.BlockSpec((HALO_ROWS, width),
                               lambda b, t: (jnp.minimum((b * tiles + t + 1) * hb, n_halo_blocks - 1), 0)),
                  pl.BlockSpec((tr, width), lambda b, t: (b * tiles + t, 0))],
        out_specs=pl.BlockSpec((tr, width), lambda b, t: (b * tiles + t, 0)),
        out_shape=jax.ShapeDtypeStruct((r, width), BF16),
        compiler_params=_params(("arbitrary",) * 2, blocks, 10 * tr * gdim * 4),
        name="pool_mix",
    )(u, u, u, g)


def _rope_tables(n_ctx, n_seq, head_dim):
    rows = n_seq // GRID_W
    row = jnp.repeat(jnp.arange(rows, dtype=F32), GRID_W)
    col = jnp.tile(jnp.arange(GRID_W, dtype=F32), rows)
    n_freq = head_dim // 4
    inv = ROPE_THETA ** (-jnp.arange(n_freq, dtype=F32) / n_freq)
    ang = jnp.concatenate([row[:, None] * inv, col[:, None] * inv], axis=-1)
    ang = jnp.concatenate([jnp.zeros((n_ctx, head_dim // 2), F32), ang], axis=0)
    cos = jnp.repeat(jnp.cos(ang), 2, axis=-1)
    sin = jnp.repeat(jnp.sin(ang), 2, axis=-1)
    sign = jnp.tile(jnp.array([-1.0, 1.0], F32), head_dim // 2)
    return cos, sin * sign


def _pair_swap_matrix(n):
    idx = jnp.arange(n)
    return (idx[:, None] == (idx[None, :] ^ 1)).astype(BF16)


def _swap_pairs(x_bf16, swap_ref):
    return jnp.dot(x_bf16, swap_ref[...], preferred_element_type=F32)


def _norm_rope(xb, cos, sin, gain, gain_swapped, swap_ref, out_scale):
    hd = ATTN_HEAD_DIM
    x = xb.astype(F32)
    partner = _swap_pairs(xb, swap_ref)
    outs = []
    for h in range(xb.shape[1] // hd):
        cols = slice(h * hd, (h + 1) * hd)
        xh = x[:, cols]
        inv = lax.rsqrt(jnp.mean(xh * xh, axis=-1, keepdims=True) + NORM_EPS) * out_scale
        outs.append(xh * inv * gain * cos + partner[:, cols] * inv * gain_swapped * sin)
    return outs[0] if len(outs) == 1 else jnp.concatenate(outs, axis=1)


def _gain_rows(g):
    hd = ATTN_HEAD_DIM
    swapped = g.reshape(hd // 2, 2)[:, ::-1].reshape(hd)
    return jnp.concatenate([g[None], swapped[None], jnp.zeros((MOD_ROWS - 2, hd), g.dtype)], axis=0).astype(F32)


def _kv_prep_kernel(k_ref, v_ref, cos_ref, sin_ref, g_ref, swap_ref, ko_ref, vt_ref):
    k = _norm_rope(k_ref[...], cos_ref[...], sin_ref[...], g_ref[0:1, :], g_ref[1:2, :], swap_ref, 1.0)
    ko_ref[...] = k.astype(ko_ref.dtype)
    vt_ref[...] = v_ref[...].astype(F32).T.astype(vt_ref.dtype)


def _kv_prep(proj, cos, sin, k_g, *, batch, n_chunks, kv_chunk, d, kv_width):
    hd = ATTN_HEAD_DIM
    kv_heads = kv_width // hd
    k_col0 = d // hd
    v_col0 = (d + kv_width) // hd
    blocks = 4 * kv_chunk * hd * 2 + 2 * kv_chunk * hd * 4 + MOD_ROWS * hd * 4 + hd * hd * 2
    return pl.pallas_call(
        _kv_prep_kernel,
        grid=(batch, kv_heads, n_chunks),
        in_specs=[pl.BlockSpec((kv_chunk, hd), lambda b, h, c: (b * n_chunks + c, k_col0 + h)),
                  pl.BlockSpec((kv_chunk, hd), lambda b, h, c: (b * n_chunks + c, v_col0 + h)),
                  pl.BlockSpec((kv_chunk, hd), lambda b, h, c: (c, 0)),
                  pl.BlockSpec((kv_chunk, hd), lambda b, h, c: (c, 0)),
                  pl.BlockSpec((MOD_ROWS, hd), lambda b, h, c: (0, 0)),
                  pl.BlockSpec((hd, hd), lambda b, h, c: (0, 0))],
        out_specs=[pl.BlockSpec((kv_chunk, hd), lambda b, h, c: (b * n_chunks + c, h)),
                   pl.BlockSpec((None, None, None, hd, kv_chunk), lambda b, h, c: (b, h, c, 0, 0))],
        out_shape=[jax.ShapeDtypeStruct((proj.shape[0], kv_width), BF16),
                   jax.ShapeDtypeStruct((batch, kv_heads, n_chunks, hd, kv_chunk), BF16)],
        compiler_params=_params(("arbitrary",) * 3, blocks, 10 * kv_chunk * hd * 4),
        name="kv_norm_rope_transpose",
    )(proj, proj, cos, sin, _gain_rows(k_g), _pair_swap_matrix(hd))


def _attn_kernel(q_ref, k_ref, vt_ref, g_ref, cos_ref, sin_ref, qg_ref, swap_ref, o_ref,
                 qs_ref, s_ref, smax_ref, m_ref, l_ref, acc_ref, *,
                 groups, ctx_tiles, ctx_rows, n_chunks, kv_chunk, q_scale):
    t = pl.program_id(2)
    hd = ATTN_HEAD_DIM
    nt = (((1,), (1,)), ((), ()))

    qs_ref[...] = _norm_rope(q_ref[...], cos_ref[...], sin_ref[...], qg_ref[0:1, :], qg_ref[1:2, :],
                             swap_ref, q_scale).astype(qs_ref.dtype)

    m_ref[...] = jnp.full(m_ref.shape, -jnp.inf, F32)
    l_ref[...] = jnp.zeros(l_ref.shape, F32)
    acc_ref[...] = jnp.zeros(acc_ref.shape, F32)

    def scores(k, slot, n, h):
        s = lax.dot_general(k, qs_ref[:, h * hd:(h + 1) * hd], nt, preferred_element_type=F32)
        s_ref[slot, h, 0:n, :] = s
        smax_ref[slot, h] = jnp.max(s, axis=0, keepdims=True)

    def softmax_pv(vt, slot, n, h):
        s = s_ref[slot, h, 0:n, :]
        m_prev = m_ref[h]
        m_new = jnp.maximum(m_prev, smax_ref[slot, h])
        alpha = jnp.exp2(m_prev - m_new)
        p = jnp.exp2(s - m_new)
        l_ref[h] = alpha * l_ref[h] + jnp.sum(p, axis=0, keepdims=True)
        m_ref[h] = m_new
        acc_ref[h] = alpha * acc_ref[h] + jnp.dot(vt, p.astype(BF16), preferred_element_type=F32)

    def chunk_step(k_next, vt, slot, n):
        ahead = min(ATTN_SCORES_AHEAD, groups)
        if k_next is not None:
            for h in range(ahead):
                scores(k_next, 1 - slot, n, h)
        for h in range(groups):
            softmax_pv(vt, slot, n, h)
            if k_next is not None and h + ahead < groups:
                scores(k_next, 1 - slot, n, h + ahead)

    def k_chunk(c):
        if isinstance(c, int):
            return k_ref[c * kv_chunk:(c + 1) * kv_chunk, :]
        return k_ref[pl.ds(pl.multiple_of(c * kv_chunk, kv_chunk), kv_chunk), :]

    @pl.when(t < ctx_tiles)
    def _():
        for r0 in range(0, ctx_rows, ROW_TILE):
            c, off = divmod(r0, kv_chunk)
            for h in range(groups):
                scores(k_ref[r0:r0 + ROW_TILE, :], 0, ROW_TILE, h)
            chunk_step(None, vt_ref[c, :, off:off + ROW_TILE], 0, ROW_TILE)

    @pl.when(t >= ctx_tiles)
    def _():
        for h in range(groups):
            scores(k_chunk(0), 0, kv_chunk, h)
        pairs = (n_chunks - 1) // 2

        def body(i, carry):
            c = 2 * i
            chunk_step(k_chunk(c + 1), vt_ref[c], 0, kv_chunk)
            chunk_step(k_chunk(c + 2), vt_ref[c + 1], 1, kv_chunk)
            return carry
        if pairs <= ATTN_UNROLL_PAIRS:
            for i in range(pairs):
                body(i, 0)
        else:
            lax.fori_loop(0, pairs, body, 0)
        if n_chunks % 2 == 0:
            chunk_step(k_chunk(n_chunks - 1), vt_ref[n_chunks - 2], 0, kv_chunk)
            chunk_step(None, vt_ref[n_chunks - 1], 1, kv_chunk)
        else:
            chunk_step(None, vt_ref[n_chunks - 1], 0, kv_chunk)

    o = jnp.concatenate([(acc_ref[h] * (1.0 / l_ref[h])).T for h in range(groups)], axis=1)
    o_ref[...] = (o * _silu(g_ref[...].astype(F32))).astype(o_ref.dtype)


def _attention(proj, cos, sin, q_g, k_g, *, batch, tiles, ctx_tiles, d, kv_width):
    r = proj.shape[0]
    hd = ATTN_HEAD_DIM
    kv_heads = kv_width // hd
    groups = d // kv_width
    bw = groups * hd
    kv_rows = tiles * ROW_TILE
    kv_chunk = _largest_tile(kv_rows, (768, 512, 256))
    n_chunks = kv_rows // kv_chunk
    keys, vt = _kv_prep(proj, cos, sin, k_g, batch=batch, n_chunks=n_chunks, kv_chunk=kv_chunk,
                        d=d, kv_width=kv_width)
    kern = functools.partial(_attn_kernel, groups=groups, ctx_tiles=ctx_tiles, ctx_rows=ctx_tiles * ROW_TILE,
                             n_chunks=n_chunks, kv_chunk=kv_chunk, q_scale=hd ** -0.5 * math.log2(math.e))
    blocks = (3 * ROW_TILE * bw * 2 + 2 * kv_rows * hd * 2 + 2 * ROW_TILE * hd * 4 + MOD_ROWS * hd * 4
              + bw * bw * 2)
    temps = (2 * groups + 6) * kv_chunk * ROW_TILE * 4 + groups * (hd + 16) * ROW_TILE * 4 + 8 * ROW_TILE * bw * 4
    return pl.pallas_call(
        kern,
        grid=(batch, kv_heads, tiles),
        in_specs=[pl.BlockSpec((ROW_TILE, bw), lambda b, h, t: (b * tiles + t, h)),
                  pl.BlockSpec((kv_rows, hd), lambda b, h, t: (b, h)),
                  pl.BlockSpec((None, None, n_chunks, hd, kv_chunk), lambda b, h, t: (b, h, 0, 0, 0)),
                  pl.BlockSpec((ROW_TILE, bw), lambda b, h, t: (b * tiles + t, (d + 2 * kv_width) // bw + h)),
                  pl.BlockSpec((ROW_TILE, hd), lambda b, h, t: (t, 0)),
                  pl.BlockSpec((ROW_TILE, hd), lambda b, h, t: (t, 0)),
                  pl.BlockSpec((MOD_ROWS, hd), lambda b, h, t: (0, 0)),
                  pl.BlockSpec((bw, bw), lambda b, h, t: (0, 0))],
        out_specs=pl.BlockSpec((ROW_TILE, bw), lambda b, h, t: (b * tiles + t, h)),
        out_shape=jax.ShapeDtypeStruct((r, d), BF16),
        scratch_shapes=[pltpu.VMEM((ROW_TILE, bw), BF16),
                        pltpu.VMEM((2, groups, kv_chunk, ROW_TILE), F32),
                        pltpu.VMEM((2, groups, 1, ROW_TILE), F32),
                        pltpu.VMEM((groups, 1, ROW_TILE), F32),
                        pltpu.VMEM((groups, 1, ROW_TILE), F32),
                        pltpu.VMEM((groups, hd, ROW_TILE), F32)],
        compiler_params=_params(("arbitrary",) * 3, blocks, temps),
        name="attention",
    )(proj, keys, vt, proj, cos, sin, _gain_rows(q_g), _pair_swap_matrix(bw))


def _ret_chunk(q_ref, k_ref, v_ref, cos_ref, sin_ref, dec_ref, swap_ref, s_ref, *, qk_dim, hps, reverse):
    tc = ROW_TILE
    v_dim = 2 * qk_dim
    nt = (((1,), (1,)), ((), ()))
    cos, sin = cos_ref[...], sin_ref[...]
    ii = lax.broadcasted_iota(jnp.int32, (tc, tc), 0)
    jj = lax.broadcasted_iota(jnp.int32, (tc, tc), 1)
    pos = lax.broadcasted_iota(jnp.int32, (tc, 1), 0).astype(F32)
    if reverse:
        dist, keep = (jj - ii), jj > ii
        q_pow, k_pow = tc - pos, pos
    else:
        dist, keep = (ii - jj), ii >= jj
        q_pow, k_pow = pos + 1.0, tc - 1.0 - pos
    dist = jnp.maximum(dist, 0).astype(F32)

    raw = []
    for h in range(hps):
        qb = q_ref[:, h * qk_dim:(h + 1) * qk_dim]
        kb = k_ref[:, h * qk_dim:(h + 1) * qk_dim]
        raw.append((qb, kb, _swap_pairs(qb, swap_ref), _swap_pairs(kb, swap_ref)))

    heads = []
    for h, (qb, kb, q_partner, k_partner) in enumerate(raw):
        log_g = jnp.log1p(-jnp.exp(dec_ref[h]))[0:1, 0:1]
        q = qb.astype(F32) * cos + q_partner * sin
        k = (kb.astype(F32) * cos + k_partner * sin) * (qk_dim ** -0.5)
        scores = lax.dot_general(q.astype(BF16), k.astype(BF16), nt, preferred_element_type=F32)
        heads.append((log_g, q, k, scores))

    outs = []
    for h, (log_g, q, k, scores) in enumerate(heads):
        v = v_ref[:, h * v_dim:(h + 1) * v_dim]
        decay = jnp.where(keep, jnp.exp(dist * log_g), 0.0)
        lhs = jnp.concatenate([(scores * decay).astype(BF16),
                               (q * jnp.exp(q_pow * log_g)).astype(BF16)], axis=1)
        rhs = jnp.concatenate([v, s_ref[h].astype(BF16)], axis=0)
        outs.append(jnp.dot(lhs, rhs, preferred_element_type=F32))

    for h, (log_g, q, k, scores) in enumerate(heads):
        v = v_ref[:, h * v_dim:(h + 1) * v_dim]
        kt = (k * jnp.exp(k_pow * log_g)).T.astype(BF16)
        s_ref[h] = s_ref[h] * jnp.exp(tc * log_g) + jnp.dot(kt, v, preferred_element_type=F32)
    return outs


def _ret_fwd_kernel(q_ref, k_ref, v_ref, cos_ref, sin_ref, dec_ref, swap_ref, o_ref, s_ref, *, qk_dim, hps):
    @pl.when(pl.program_id(2) == 0)
    def _():
        s_ref[...] = jnp.zeros(s_ref.shape, F32)

    outs = _ret_chunk(q_ref, k_ref, v_ref, cos_ref, sin_ref, dec_ref, swap_ref, s_ref,
                      qk_dim=qk_dim, hps=hps, reverse=False)
    o_ref[...] = jnp.concatenate(outs, axis=1).astype(o_ref.dtype)


def _ret_bwd_kernel(q_ref, k_ref, v_ref, cos_ref, sin_ref, dec_ref, swap_ref, of_ref, g_ref, gn_ref, o_ref, s_ref,
                    *, qk_dim, hps):
    @pl.when(pl.program_id(2) == 0)
    def _():
        s_ref[...] = jnp.zeros(s_ref.shape, F32)

    outs = _ret_chunk(q_ref, k_ref, v_ref, cos_ref, sin_ref, dec_ref, swap_ref, s_ref,
                      qk_dim=qk_dim, hps=hps, reverse=True)
    v_dim = 2 * qk_dim
    normed = []
    for h, o in enumerate(outs):
        cols = slice(h * v_dim, (h + 1) * v_dim)
        o = o + of_ref[:, cols].astype(F32)
        mu = jnp.mean(o, axis=-1, keepdims=True)
        dev = o - mu
        var = jnp.mean(dev * dev, axis=-1, keepdims=True)
        normed.append(dev * lax.rsqrt(var + GN_EPS) * gn_ref[:, cols])
    o_ref[...] = (jnp.concatenate(normed, axis=1) * _silu(g_ref[...].astype(F32))).astype(o_ref.dtype)


def _retention(proj, cos, sin, decay_f, decay_b, gn_g, *, batch, tiles, ctx_tiles, d, heads):
    r = proj.shape[0]
    qk_dim = d // heads
    v_dim = 2 * qk_dim
    tc = ROW_TILE
    dec_f = jnp.broadcast_to(decay_f.astype(F32)[:, None, None], (heads, 8, 128))
    dec_b = jnp.broadcast_to(decay_b.astype(F32)[:, None, None], (heads, 8, 128))
    hps = _largest_tile(heads, (RET_HEADS_PER_STEP, 1))
    swap = _pair_swap_matrix(qk_dim)
    qw, vw = hps * qk_dim, hps * v_dim
    k_col0 = d // qw
    v_col0 = 2 * d // vw
    g_col0 = (2 * d + heads * v_dim) // vw
    blocks = 2 * tc * qw * 2 + 4 * tc * vw * 2 + 2 * tc * qk_dim * 4 + hps * 8 * 128 * 4 + vw * 4
    temps = hps * (qk_dim * v_dim * 4 + 12 * tc * v_dim * 4)

    def fwd_row(b, h, c):
        return b * tiles + c

    def bwd_chunk(c):
        return jnp.where(c < ctx_tiles, ctx_tiles - 1 - c, tiles - 1 - (c - ctx_tiles))

    def bwd_row(b, h, c):
        return b * tiles + bwd_chunk(c)

    o_f = pl.pallas_call(
        functools.partial(_ret_fwd_kernel, qk_dim=qk_dim, hps=hps),
        grid=(batch, heads // hps, tiles),
        in_specs=[pl.BlockSpec((tc, qw), lambda b, h, c: (fwd_row(b, h, c), h)),
                  pl.BlockSpec((tc, qw), lambda b, h, c: (fwd_row(b, h, c), k_col0 + h)),
                  pl.BlockSpec((tc, vw), lambda b, h, c: (fwd_row(b, h, c), v_col0 + h)),
                  pl.BlockSpec((tc, qk_dim), lambda b, h, c: (c, 0)),
                  pl.BlockSpec((tc, qk_dim), lambda b, h, c: (c, 0)),
                  pl.BlockSpec((hps, 8, 128), lambda b, h, c: (h, 0, 0)),
                  pl.BlockSpec((qk_dim, qk_dim), lambda b, h, c: (0, 0))],
        out_specs=pl.BlockSpec((tc, vw), lambda b, h, c: (fwd_row(b, h, c), h)),
        out_shape=jax.ShapeDtypeStruct((r, heads * v_dim), BF16),
        scratch_shapes=[pltpu.VMEM((hps, qk_dim, v_dim), F32)],
        compiler_params=_params(("arbitrary",) * 3, blocks, temps),
        name="retention_fwd",
    )(proj, proj, proj, cos, sin, dec_f, swap)

    return pl.pallas_call(
        functools.partial(_ret_bwd_kernel, qk_dim=qk_dim, hps=hps),
        grid=(batch, heads // hps, tiles),
        in_specs=[pl.BlockSpec((tc, qw), lambda b, h, c: (bwd_row(b, h, c), h)),
                  pl.BlockSpec((tc, qw), lambda b, h, c: (bwd_row(b, h, c), k_col0 + h)),
                  pl.BlockSpec((tc, vw), lambda b, h, c: (bwd_row(b, h, c), v_col0 + h)),
                  pl.BlockSpec((tc, qk_dim), lambda b, h, c: (bwd_chunk(c), 0)),
                  pl.BlockSpec((tc, qk_dim), lambda b, h, c: (bwd_chunk(c), 0)),
                  pl.BlockSpec((hps, 8, 128), lambda b, h, c: (h, 0, 0)),
                  pl.BlockSpec((qk_dim, qk_dim), lambda b, h, c: (0, 0)),
                  pl.BlockSpec((tc, vw), lambda b, h, c: (bwd_row(b, h, c), h)),
                  pl.BlockSpec((tc, vw), lambda b, h, c: (bwd_row(b, h, c), g_col0 + h)),
                  pl.BlockSpec((1, vw), lambda b, h, c: (0, h))],
        out_specs=pl.BlockSpec((tc, vw), lambda b, h, c: (bwd_row(b, h, c), h)),
        out_shape=jax.ShapeDtypeStruct((r, heads * v_dim), BF16),
        scratch_shapes=[pltpu.VMEM((hps, qk_dim, v_dim), F32)],
        compiler_params=_params(("arbitrary",) * 3, blocks, temps),
        name="retention_bwd_norm_gate",
    )(proj, proj, proj, cos, sin, dec_b, swap, o_f, proj, gn_g.astype(F32).reshape(1, heads * v_dim))


def _final_norm_kernel(x_ref, g_ref, o_ref):
    x = x_ref[...]
    o_ref[...] = x * lax.rsqrt(jnp.mean(x * x, axis=-1, keepdims=True) + NORM_EPS) * g_ref[...]


def _final_norm(xs, g, *, batch, tiles, ctx_tiles):
    d = xs.shape[1]
    seq_tiles = tiles - ctx_tiles
    blocks = 2 * ROW_TILE * d * 4 + d * 4
    return pl.pallas_call(
        _final_norm_kernel,
        grid=(batch, seq_tiles),
        in_specs=[pl.BlockSpec((ROW_TILE, d), lambda b, t: (b * tiles + ctx_tiles + t, 0)),
                  pl.BlockSpec((1, d), lambda b, t: (0, 0))],
        out_specs=pl.BlockSpec((ROW_TILE, d), lambda b, t: (b * seq_tiles + t, 0)),
        out_shape=jax.ShapeDtypeStruct((batch * seq_tiles * ROW_TILE, d), F32),
        compiler_params=_params(("arbitrary", "arbitrary"), blocks, 2 * ROW_TILE * d * 4),
        name="final_norm",
    )(xs, g.reshape(1, d))


def kernel(x, c, ctx, c_ctx, norm_g, mod_w, mod_b, pool_w_in, pool_w_grp, pool_scale, pool_w_out,
           attn_w_in, attn_q_norm, attn_k_norm, attn_w_out, ret_w_in, ret_decay_fwd, ret_decay_bwd,
           ret_gn_g, ret_w_out, final_norm_g):
    batch, seq, d = x.shape
    n_ctx = ctx.shape[1]
    depth = norm_g.shape[0]
    assert seq % ROW_TILE == 0 and n_ctx % ROW_TILE == 0 and seq % GRID_W == 0
    assert batch < MOD_ROWS
    ctx_tiles = n_ctx // ROW_TILE
    tiles = (n_ctx + seq) // ROW_TILE
    rows_per_sample = n_ctx + seq
    tm = _largest_tile(rows_per_sample, (768, 512, 256))

    cvec = jnp.concatenate([c, c_ctx[None, :], jnp.zeros((MOD_ROWS - batch - 1, d), F32)], axis=0)
    mods = _mod_tables(cvec, mod_w, mod_b)
    geom = dict(batch=batch, tiles=tiles, ctx_tiles=ctx_tiles)

    kv_width = (attn_w_in.shape[2] - 2 * d) // 2
    ret_heads = ret_decay_fwd.shape[1]
    attn_cos, attn_sin = _rope_tables(n_ctx, seq, ATTN_HEAD_DIM)
    ret_cos, ret_sin = _rope_tables(n_ctx, seq, d // ret_heads)

    pool_w_out, attn_w_out, ret_w_out = (w.astype(BF16) for w in (pool_w_out, attn_w_out, ret_w_out))
    for i in range(depth):
        kind, j = i % 3, i // 3
        if i == 0:
            h, xs = _join_norm_mod(ctx, x, norm_g[i], mods[i], **geom)
        else:
            h = _norm_mod(xs, norm_g[i], mods[i], **geom)
        if kind == 0:
            width = pool_w_in.shape[2] // 2
            u = _matmul(h, _fold_group_mix(pool_w_in, pool_w_grp, pool_scale, j)[None], 0, name="in_proj_mixed")
            g = _matmul(h, pool_w_in, j, col0=width, n_cols=width, gate=True)
            z = _pool_mix(u, g, n_groups=pool_w_grp.shape[1], **geom)
            w_out = pool_w_out
        elif kind == 1:
            proj = _matmul(h, attn_w_in, j)
            z = _attention(proj, attn_cos, attn_sin, attn_q_norm[j], attn_k_norm[j],
                           **geom, d=d, kv_width=kv_width)
            w_out = attn_w_out
        else:
            proj = _matmul(h, ret_w_in, j)
            z = _retention(proj, ret_cos, ret_sin, ret_decay_fwd[j], ret_decay_bwd[j], ret_gn_g[j],
                           **geom, d=d, heads=ret_heads)
            w_out = ret_w_out
        xs = _matmul_residual(z, w_out, j, xs, mods[i], batch=batch, tm=tm, ctx_rows=n_ctx)

    out = _final_norm(xs, final_norm_g, **geom)
    return out.reshape(batch, seq, d)
```
